```python
import jax
import jax.numpy as jnp
from jax import lax
import numpy as np

D_MODEL = 4096
BATCH = 2
SEQ = 8192
DEPTH = 2

N_MIXERS = 2
D_FF = 4 * D_MODEL
NORM_EPS = 1e-6
LN_EPS = 1e-5

SG_CHUNK = 128
SG_WIDTH = 2 * D_MODEL
SG_GROUPS = 32
SG_GROUP_DIM = SG_WIDTH // SG_GROUPS

DN_HEAD_DIM = 128
DN_K_HEADS = D_MODEL // 128
DN_V_HEADS = 2 * DN_K_HEADS
DN_K_DIM = DN_K_HEADS * DN_HEAD_DIM
DN_V_DIM = DN_V_HEADS * DN_HEAD_DIM
DN_CONV = 4
DN_CHUNK = 64
DN_CONV_CH = 2 * DN_K_DIM + DN_V_DIM
DN_IN = DN_CONV_CH + DN_V_DIM + 2 * DN_V_HEADS

N_A_LAYERS = (DEPTH + 1) // 2
N_B_LAYERS = DEPTH // 2

kernel_name = "hybrid_sgmlp_gated_deltanet_sandwich"


def rms_norm(x, w):
    xf = x.astype(jnp.float32)
    y = xf * lax.rsqrt(jnp.mean(xf * xf, axis=-1, keepdims=True) + NORM_EPS)
    return (y * w.astype(jnp.float32)).astype(x.dtype)


def layer_norm(x, w, b):
    xf = x.astype(jnp.float32)
    mu = jnp.mean(xf, axis=-1, keepdims=True)
    xc = xf - mu
    y = xc * lax.rsqrt(jnp.mean(xc * xc, axis=-1, keepdims=True) + LN_EPS)
    return (y * w.astype(jnp.float32) + b.astype(jnp.float32)).astype(x.dtype)


def spatial_gating_mixer(h, w_in, ln_w, ln_b, w_s, b_s, w_out):
    B, T, _ = h.shape
    n = T // SG_CHUNK
    z = jax.nn.gelu(h @ w_in, approximate=False)
    u, v = jnp.split(z, 2, axis=-1)
    v = layer_norm(v, ln_w, ln_b)
    v = v.reshape(B, n, SG_CHUNK, SG_GROUPS, SG_GROUP_DIM)
    causal = jnp.tril(jnp.ones((SG_CHUNK, SG_CHUNK), dtype=bool))
    w_c = jnp.where(causal, w_s, jnp.zeros_like(w_s)).astype(v.dtype)
    s = jnp.einsum("gts,bnsgc->bntgc", w_c, v) + b_s.T.astype(v.dtype)[None, None, :, :, None]
    s = s.reshape(B, T, SG_WIDTH)
    return (u * s) @ w_out


def l2_normalize(x):
    return x * lax.rsqrt(jnp.sum(x * x, axis=-1, keepdims=True) + 1e-6)


def causal_depthwise_conv(x, w):
    y = lax.conv_general_dilated(
        x, w[:, None, :].astype(x.dtype), window_strides=(1,),
        padding=((DN_CONV - 1, 0),), dimension_numbers=("NWC", "WIO", "NWC"),
        feature_group_count=x.shape[-1])
    return jax.nn.silu(y)


def chunk_gated_delta_rule(q, k, v, g, beta):
    B, T, H, dk = q.shape
    dv = v.shape[-1]
    C = DN_CHUNK
    n = T // C
    q = l2_normalize(q) * (dk ** -0.5)
    k = l2_normalize(k)

    def to_chunks(a):
        a = a.reshape((B, n, C, H) + a.shape[3:])
        perm = (1, 0, 3, 2) + tuple(range(4, a.ndim))
        return jnp.transpose(a, perm)

    tril = jnp.tril(jnp.ones((C, C), dtype=bool))
    strict = jnp.tril(jnp.ones((C, C), dtype=bool), -1)
    eye = jnp.eye(C, dtype=jnp.float32)

    def step(S, inp):
        q_c, k_c, v_c, g_c, b_c = inp
        g_c = jnp.cumsum(g_c, axis=-1)
        decay = jnp.exp(jnp.where(tril, g_c[..., :, None] - g_c[..., None, :], -jnp.inf))
        k_beta = k_c * b_c[..., None]
        kk = jnp.einsum("bhtd,bhsd->bhts", k_beta, k_c)
        a_mat = eye + jnp.where(strict, kk * decay, 0.0)
        rhs = jnp.concatenate([v_c * b_c[..., None], k_beta * jnp.exp(g_c)[..., None]], axis=-1)
        sol = lax.linalg.triangular_solve(a_mat, rhs, left_side=True, lower=True, unit_diagonal=True)
        u_c, w_c = sol[..., :dv], sol[..., dv:]
        v_new = u_c - jnp.einsum("bhtk,bhkv->bhtv", w_c, S)
        attn = jnp.einsum("bhtd,bhsd->bhts", q_c, k_c) * decay
        o = (jnp.einsum("bhtk,bhkv->bhtv", q_c * jnp.exp(g_c)[..., None], S)
             + jnp.einsum("bhts,bhsv->bhtv", attn, v_new))
        g_last = g_c[..., -1]
        S = (S * jnp.exp(g_last)[..., None, None]
             + jnp.einsum("bhtk,bhtv->bhkv", k_c * jnp.exp(g_last[..., None] - g_c)[..., None], v_new))
        return S, o

    S0 = jnp.zeros((B, H, dk, dv), dtype=jnp.float32)
    _, o = lax.scan(step, S0, (to_chunks(q), to_chunks(k), to_chunks(v), to_chunks(g), to_chunks(beta)))
    return jnp.transpose(o, (1, 0, 3, 2, 4)).reshape(B, T, H, dv)


def gated_deltanet_mixer(h, w_in, conv_w, a_log, dt_bias, o_norm_w, w_out):
    B, T, _ = h.shape
    proj = h @ w_in
    qkv, z, b_raw, a_raw = jnp.split(
        proj, [DN_CONV_CH, DN_CONV_CH + DN_V_DIM, DN_CONV_CH + DN_V_DIM + DN_V_HEADS], axis=-1)
    qkv = causal_depthwise_conv(qkv, conv_w)
    q, k, v = jnp.split(qkv, [DN_K_DIM, 2 * DN_K_DIM], axis=-1)
    rep = DN_V_HEADS // DN_K_HEADS
    q = jnp.repeat(q.reshape(B, T, DN_K_HEADS, DN_HEAD_DIM), rep, axis=2).astype(jnp.float32)
    k = jnp.repeat(k.reshape(B, T, DN_K_HEADS, DN_HEAD_DIM), rep, axis=2).astype(jnp.float32)
    v = v.reshape(B, T, DN_V_HEADS, DN_HEAD_DIM).astype(jnp.float32)
    beta = jax.nn.sigmoid(b_raw.astype(jnp.float32))
    g = -jnp.exp(a_log.astype(jnp.float32)) * jax.nn.softplus(
        a_raw.astype(jnp.float32) + dt_bias.astype(jnp.float32))
    o = chunk_gated_delta_rule(q, k, v, g, beta)
    zf = z.reshape(B, T, DN_V_HEADS, DN_HEAD_DIM).astype(jnp.float32)
    o = (o * lax.rsqrt(jnp.mean(o * o, axis=-1, keepdims=True) + NORM_EPS)
         * o_norm_w.astype(jnp.float32) * jax.nn.silu(zf))
    return o.reshape(B, T, DN_V_DIM).astype(h.dtype) @ w_out


def squared_relu_mlp(h, w_up, w_down):
    return jnp.square(jax.nn.relu(h @ w_up)) @ w_down


def setup_inputs(seed: int = 0) -> dict:
    key = jax.random.key(seed)
    ks = jax.random.split(key, 20)
    f32 = jnp.float32
    nrm = lambda k, shape, scale: jax.random.normal(k, shape, dtype=f32) * scale
    x = nrm(ks[0], (BATCH, SEQ, D_MODEL), 1.0)
    norm_w = 1.0 + nrm(ks[1], (DEPTH, 4, D_MODEL), 0.02)
    sg_w_in = nrm(ks[2], (N_A_LAYERS, D_MODEL, 2 * SG_WIDTH), D_MODEL ** -0.5)
    sg_ln_w = 1.0 + nrm(ks[3], (N_A_LAYERS, SG_WIDTH), 0.02)
    sg_ln_b = nrm(ks[4], (N_A_LAYERS, SG_WIDTH), 0.02)
    sg_w_s = nrm(ks[5], (N_A_LAYERS, SG_GROUPS, SG_CHUNK, SG_CHUNK), SG_CHUNK ** -0.5)
    sg_b_s = 1.0 + nrm(ks[6], (N_A_LAYERS, SG_GROUPS, SG_CHUNK), 0.02)
    sg_w_out = nrm(ks[7], (N_A_LAYERS, SG_WIDTH, D_MODEL), SG_WIDTH ** -0.5)
    dn_w_in = nrm(ks[8], (N_B_LAYERS, D_MODEL, DN_IN), D_MODEL ** -0.5)
    dn_conv_w = nrm(ks[9], (N_B_LAYERS, DN_CONV, DN_CONV_CH), DN_CONV ** -0.5)
    dn_a_log = jnp.log(jax.random.uniform(ks[10], (N_B_LAYERS, DN_V_HEADS), dtype=f32, minval=1.0, maxval=16.0))
    dt = jnp.exp(jax.random.uniform(ks[11], (N_B_LAYERS, DN_V_HEADS), dtype=f32,
                                    minval=float(np.log(1e-3)), maxval=float(np.log(1e-1))))
    dn_dt_bias = dt + jnp.log(-jnp.expm1(-dt))
    dn_norm_w = 1.0 + nrm(ks[12], (N_B_LAYERS, DN_HEAD_DIM), 0.02)
    dn_w_out = nrm(ks[13], (N_B_LAYERS, DN_V_DIM, D_MODEL), DN_V_DIM ** -0.5)
    mlp_w_up = nrm(ks[14], (DEPTH, D_MODEL, D_FF), D_MODEL ** -0.5)
    mlp_w_down = nrm(ks[15], (DEPTH, D_FF, D_MODEL), D_FF ** -0.5)
    return {"x": x, "norm_w": norm_w,
            "sg_w_in": sg_w_in, "sg_ln_w": sg_ln_w, "sg_ln_b": sg_ln_b, "sg_w_s": sg_w_s,
            "sg_b_s": sg_b_s, "sg_w_out": sg_w_out,
            "dn_w_in": dn_w_in, "dn_conv_w": dn_conv_w, "dn_a_log": dn_a_log, "dn_dt_bias": dn_dt_bias,
            "dn_norm_w": dn_norm_w, "dn_w_out": dn_w_out,
            "mlp_w_up": mlp_w_up, "mlp_w_down": mlp_w_down}


def reference(x, norm_w, sg_w_in, sg_ln_w, sg_ln_b, sg_w_s, sg_b_s, sg_w_out,
              dn_w_in, dn_conv_w, dn_a_log, dn_dt_bias, dn_norm_w, dn_w_out,
              mlp_w_up, mlp_w_down):
    h = x
    for i in range(DEPTH):
        j = i // N_MIXERS
        t = rms_norm(h, norm_w[i, 0])
        if i % N_MIXERS == 0:
            m = spatial_gating_mixer(t, sg_w_in[j], sg_ln_w[j], sg_ln_b[j], sg_w_s[j], sg_b_s[j], sg_w_out[j])
        else:
            m = gated_deltanet_mixer(t, dn_w_in[j], dn_conv_w[j], dn_a_log[j], dn_dt_bias[j],
                                     dn_norm_w[j], dn_w_out[j])
        h = h + rms_norm(m, norm_w[i, 1])
        t = rms_norm(h, norm_w[i, 2])
        f = squared_relu_mlp(t, mlp_w_up[i], mlp_w_down[i])
        h = h + rms_norm(f, norm_w[i, 3])
    return h
```

```python
import functools
import math

import jax
import jax.numpy as jnp
from jax import lax
from jax.experimental import pallas as pl
from jax.experimental.pallas import tpu as pltpu

NORM_EPS = 1e-6
LN_EPS = 1e-5
L2_EPS = 1e-6
SG_CHUNK = 128
DN_CHUNK = 64
DN_CONV = 4
CONV_HALO = 8
LANES = 128
V7X_VMEM_LIMIT_BYTES = 60000 * 1024

F32 = jnp.float32
BF16 = jnp.bfloat16


def _tile(dim, pref, align=LANES):
    if dim <= pref:
        return dim
    t = (pref // align) * align
    while t > align and dim % t:
        t -= align
    assert dim % t == 0, (dim, pref)
    return t


def _params(sem):
    return pltpu.CompilerParams(dimension_semantics=sem, vmem_limit_bytes=V7X_VMEM_LIMIT_BYTES)


def _sigmoid(x):
    return 1.0 / (1.0 + jnp.exp(-x))


def _softplus(x):
    return jnp.maximum(x, 0.0) + jnp.log1p(jnp.exp(-jnp.abs(x)))


def _rms_cast_kernel(x_ref, w_ref, o_ref):
    x = x_ref[...]
    ms = jnp.mean(x * x, axis=-1, keepdims=True)
    o_ref[...] = (x * lax.rsqrt(ms + NORM_EPS) * w_ref[...]).astype(o_ref.dtype)


def _rms_cast(x, w):
    m, d = x.shape
    tm = _tile(m, 256, 8)
    return pl.pallas_call(
        _rms_cast_kernel,
        grid=(m // tm,),
        in_specs=[pl.BlockSpec((tm, d), lambda i: (i, 0)),
                  pl.BlockSpec((1, d), lambda i: (0, 0))],
        out_specs=pl.BlockSpec((tm, d), lambda i: (i, 0)),
        out_shape=jax.ShapeDtypeStruct((m, d), BF16),
        compiler_params=_params(("parallel",)),
        name="rms_cast",
    )(x, w.reshape(1, d))


def _mm_act_kernel(x_ref, w_ref, o_ref, *, act):
    acc = jnp.dot(x_ref[...], w_ref[...], preferred_element_type=F32)
    if act == "gelu":
        acc = 0.5 * acc * (1.0 + lax.erf(acc * math.sqrt(0.5)))
    elif act == "relu2":
        r = jnp.maximum(acc, 0.0)
        acc = r * r
    o_ref[...] = acc.astype(o_ref.dtype)


def _mm_act(x, w, act, name):
    m, k = x.shape
    n = w.shape[1]
    tm = _tile(m, 1024)
    tn = _tile(n, 1024)
    return pl.pallas_call(
        functools.partial(_mm_act_kernel, act=act),
        grid=(m // tm, n // tn),
        in_specs=[pl.BlockSpec((tm, k), lambda i, j: (i, 0)),
                  pl.BlockSpec((k, tn), lambda i, j: (0, j))],
        out_specs=pl.BlockSpec((tm, tn), lambda i, j: (i, j)),
        out_shape=jax.ShapeDtypeStruct((m, n), BF16),
        compiler_params=_params(("parallel", "arbitrary")),
        name=name,
    )(x, w)


def _mm_resnorm_kernel(x_ref, w_ref, res_ref, wpost_ref, wnext_ref, h_ref, *t_ref, nk, col_chunk, row_chunk):
    kk = pl.program_id(1)
    tm, n = h_ref.shape

    @pl.when(kk == 0)
    def _():
        h_ref[...] = jnp.zeros_like(h_ref)

    x = x_ref[...]
    for c in range(n // col_chunk):
        sl = slice(c * col_chunk, (c + 1) * col_chunk)
        h_ref[:, sl] += jnp.dot(x, w_ref[:, sl], preferred_element_type=F32)

    @pl.when(kk == nk - 1)
    def _():
        def body(r, carry):
            rows = pl.ds(pl.multiple_of(r * row_chunk, row_chunk), row_chunk)
            mval = h_ref[rows, :]
            ms = jnp.mean(mval * mval, axis=-1, keepdims=True)
            h = res_ref[rows, :] + mval * lax.rsqrt(ms + NORM_EPS) * wpost_ref[...]
            h_ref[rows, :] = h
            if t_ref:
                hs = jnp.mean(h * h, axis=-1, keepdims=True)
                t_ref[0][rows, :] = (h * lax.rsqrt(hs + NORM_EPS) * wnext_ref[...]).astype(BF16)
            return carry

        lax.fori_loop(0, tm // row_chunk, body, 0)


def _mm_resnorm(x, w, res, w_post, w_next, name):
    m, k = x.shape
    n = w.shape[1]
    tm = _tile(m, 512)
    tk = _tile(k, 512)
    nk = k // tk
    emit_t = w_next is not None
    wn = (w_next if emit_t else w_post).reshape(1, n)
    out_shape = [jax.ShapeDtypeStruct((m, n), F32)]
    out_specs = [pl.BlockSpec((tm, n), lambda i, j: (i, 0))]
    if emit_t:
        out_shape.append(jax.ShapeDtypeStruct((m, n), BF16))
        out_specs.append(pl.BlockSpec((tm, n), lambda i, j: (i, 0)))
    outs = pl.pallas_call(
        functools.partial(_mm_resnorm_kernel, nk=nk, col_chunk=_tile(n, 1024), row_chunk=_tile(tm, 32, 8)),
        grid=(m // tm, nk),
        in_specs=[pl.BlockSpec((tm, tk), lambda i, j: (i, j)),
                  pl.BlockSpec((tk, n), lambda i, j: (j, 0)),
                  pl.BlockSpec((tm, n), lambda i, j: (i, 0)),
                  pl.BlockSpec((1, n), lambda i, j: (0, 0)),
                  pl.BlockSpec((1, n), lambda i, j: (0, 0))],
        out_specs=out_specs,
        out_shape=out_shape,
        compiler_params=_params(("parallel", "arbitrary")),
        name=name,
    )(x, w, res, w_post.reshape(1, n), wn)
    return (outs[0], outs[1]) if emit_t else (outs[0], None)


def _sgu_kernel(u_ref, v_ref, lnw_ref, lnb_ref, ws_ref, bs_ref, o_ref, vn_ref, *, groups, row_chunk):
    tm, e = v_ref.shape
    gd = e // groups

    def ln_body(r, carry):
        rows = pl.ds(pl.multiple_of(r * row_chunk, row_chunk), row_chunk)
        v = v_ref[rows, :].astype(F32)
        mu = jnp.mean(v, axis=-1, keepdims=True)
        xc = v - mu
        var = jnp.mean(xc * xc, axis=-1, keepdims=True)
        vn_ref[rows, :] = (xc * lax.rsqrt(var + LN_EPS) * lnw_ref[...] + lnb_ref[...]).astype(BF16)
        return carry

    lax.fori_loop(0, tm // row_chunk, ln_body, 0)

    ri = lax.broadcasted_iota(jnp.int32, (SG_CHUNK, SG_CHUNK), 0)
    ci = lax.broadcasted_iota(jnp.int32, (SG_CHUNK, SG_CHUNK), 1)
    causal = ri >= ci

    def g_body(g, carry):
        w = jnp.where(causal, ws_ref[g], jnp.zeros((), BF16))
        bias = bs_ref[g]
        cols = pl.ds(pl.multiple_of(g * gd, gd), gd)
        for c in range(tm // SG_CHUNK):
            rows = slice(c * SG_CHUNK, (c + 1) * SG_CHUNK)
            s = jnp.dot(w, vn_ref[rows, cols], preferred_element_type=F32) + bias
            o_ref[rows, cols] = (u_ref[rows, cols].astype(F32) * s).astype(BF16)
        return carry

    lax.fori_loop(0, groups, g_body, 0)


def _sgu(z, ln_w, ln_b, w_s, b_s):
    m, e2 = z.shape
    e = e2 // 2
    groups, c, _ = w_s.shape
    assert c == SG_CHUNK
    tm = _tile(m, 256, SG_CHUNK)
    return pl.pallas_call(
        functools.partial(_sgu_kernel, groups=groups, row_chunk=_tile(tm, 32, 8)),
        grid=(m // tm,),
        in_specs=[pl.BlockSpec((tm, e), lambda i: (i, 0)),
                  pl.BlockSpec((tm, e), lambda i: (i, 1)),
                  pl.BlockSpec((1, e), lambda i: (0, 0)),
                  pl.BlockSpec((1, e), lambda i: (0, 0)),
                  pl.BlockSpec((groups, c, c), lambda i: (0, 0, 0)),
                  pl.BlockSpec((groups, c, 1), lambda i: (0, 0, 0))],
        out_specs=pl.BlockSpec((tm, e), lambda i: (i, 0)),
        out_shape=jax.ShapeDtypeStruct((m, e), BF16),
        scratch_shapes=[pltpu.VMEM((tm, e), BF16)],
        compiler_params=_params(("parallel",)),
        name="sgu",
    )(z, z, ln_w.reshape(1, e), ln_b.reshape(1, e), w_s.astype(BF16), b_s.reshape(groups, c, 1))


def _chunk_cumsum(x, axis):
    pos = lax.broadcasted_iota(jnp.int32, x.shape, axis) % DN_CHUNK
    sh = 1
    while sh < DN_CHUNK:
        x = x + jnp.where(pos >= sh, pltpu.roll(x, sh, axis), 0.0)
        sh *= 2
    return x


def _gates_kernel(t_ref, wc_ref, wr_ref, ac_ref, dc_ref, ar_ref, dr_ref, gcol_ref, grow_ref, *, heads):
    t = t_ref[...]
    ba = jnp.dot(t, wc_ref[...], preferred_element_type=F32)
    beta = _sigmoid(ba)
    g = -jnp.exp(ac_ref[...]) * _softplus(ba + dc_ref[...])
    lane = lax.broadcasted_iota(jnp.int32, ba.shape, 1)
    gcol_ref[...] = jnp.where(lane < heads, beta, _chunk_cumsum(g, 0))
    ar = lax.dot_general(wr_ref[...], t, (((1,), (1,)), ((), ())), preferred_element_type=F32)
    gr = -jnp.exp(ar_ref[...]) * _softplus(ar + dr_ref[...])
    grow_ref[...] = _chunk_cumsum(gr, 1)


def _gates(t, w_ba, a_log, dt_bias):
    m, d = t.shape
    heads = a_log.shape[0]
    tm = _tile(m, 512)
    zeros = jnp.zeros((heads,), F32)
    ac = jnp.concatenate([zeros, a_log]).reshape(1, 2 * heads)
    dc = jnp.concatenate([zeros, dt_bias]).reshape(1, 2 * heads)
    wc = w_ba.astype(BF16)
    wr = w_ba[:, heads:].T.astype(BF16)
    return pl.pallas_call(
        functools.partial(_gates_kernel, heads=heads),
        grid=(m // tm,),
        in_specs=[pl.BlockSpec((tm, d), lambda i: (i, 0)),
                  pl.BlockSpec((d, 2 * heads), lambda i: (0, 0)),
                  pl.BlockSpec((heads, d), lambda i: (0, 0)),
                  pl.BlockSpec((1, 2 * heads), lambda i: (0, 0)),
                  pl.BlockSpec((1, 2 * heads), lambda i: (0, 0)),
                  pl.BlockSpec((heads, 1), lambda i: (0, 0)),
                  pl.BlockSpec((heads, 1), lambda i: (0, 0))],
        out_specs=[pl.BlockSpec((tm, 2 * heads), lambda i: (i, 0)),
                   pl.BlockSpec((heads, tm), lambda i: (0, i))],
        out_shape=[jax.ShapeDtypeStruct((m, 2 * heads), F32),
                   jax.ShapeDtypeStruct((heads, m), F32)],
        compiler_params=_params(("parallel",)),
        name="dn_gates",
    )(t, wc, wr, ac, dc, a_log.reshape(heads, 1), dt_bias.reshape(heads, 1))


def _delta_kernel(q_ref, k_ref, v_ref, z_ref, gcol_ref, grow_ref, cwq_ref, cwk_ref, cwv_ref, nw_ref,
                  o_ref, xpad_ref, s_ref, *, heads, rep):
    j = pl.program_id(1)
    n = pl.program_id(2)
    tb, hd = q_ref.shape
    width = (2 + rep) * hd
    c_len = DN_CHUNK

    @pl.when(n == 0)
    def _():
        xpad_ref[0:CONV_HALO, :] = jnp.zeros((CONV_HALO, width), F32)
        s_ref[...] = jnp.zeros_like(s_ref)

    @pl.when(n > 0)
    def _():
        xpad_ref[0:CONV_HALO, :] = xpad_ref[tb:tb + CONV_HALO, :]

    xpad_ref[CONV_HALO:, 0:hd] = q_ref[...].astype(F32)
    xpad_ref[CONV_HALO:, hd:2 * hd] = k_ref[...].astype(F32)
    xpad_ref[CONV_HALO:, 2 * hd:] = v_ref[...].astype(F32)

    cw = jnp.concatenate([cwq_ref[...], cwk_ref[...], cwv_ref[...]], axis=1)
    y = jnp.zeros((tb, width), F32)
    for i in range(DN_CONV):
        y = y + cw[i:i + 1, :] * xpad_ref[pl.ds(CONV_HALO - (DN_CONV - 1) + i, tb), :]
    y = y * _sigmoid(y)
    q = y[:, 0:hd]
    k = y[:, hd:2 * hd]
    q = q * lax.rsqrt(jnp.sum(q * q, axis=-1, keepdims=True) + L2_EPS) * (hd ** -0.5)
    k = k * lax.rsqrt(jnp.sum(k * k, axis=-1, keepdims=True) + L2_EPS)

    shift = (2 * heads - rep * j) % (2 * heads)
    gsel = pltpu.roll(gcol_ref[...], shift, 1)
    grow = pltpu.roll(grow_ref[...], (8 - (rep * j) % 8) % 8, 0)[0:rep]

    ri = lax.broadcasted_iota(jnp.int32, (c_len, c_len), 0)
    ci = lax.broadcasted_iota(jnp.int32, (c_len, c_len), 1)
    tril = ri >= ci
    strict = ri > ci
    nw = nw_ref[...]

    states = [s_ref[i] for i in range(rep)]
    for c in range(tb // c_len):
        rows = slice(c * c_len, (c + 1) * c_len)
        qc = q[rows]
        kc = k[rows]
        qb = qc.astype(BF16)
        kb = kc.astype(BF16)
        qk = pl.dot(qb, kb, trans_b=True)
        kk = pl.dot(kb, kb, trans_b=True)
        for i in range(rep):
            bcol = gsel[rows, i:i + 1]
            gcol = gsel[rows, heads + i:heads + i + 1]
            gr = grow[i:i + 1, rows]
            dec = jnp.where(tril, jnp.exp(jnp.where(tril, gcol - gr, 0.0)), 0.0)
            a = jnp.where(strict, bcol * kk * dec, 0.0)
            p = -a
            pw = a
            for _ in range(int(math.log2(c_len)) - 1):
                pwb = pw.astype(BF16)
                pw = jnp.dot(pwb, pwb, preferred_element_type=F32)
                p = p + pw + jnp.dot(p.astype(BF16), pw.astype(BF16), preferred_element_type=F32)
            eg = jnp.exp(gcol)
            vc = y[rows, (2 + i) * hd:(3 + i) * hd]
            rhs = jnp.concatenate([bcol * vc, (bcol * eg) * kc], axis=1)
            sol = rhs + jnp.dot(p.astype(BF16), rhs.astype(BF16), preferred_element_type=F32)
            u = sol[:, 0:hd]
            w = sol[:, hd:]
            sb = states[i].astype(BF16)
            v_new = u - jnp.dot(w.astype(BF16), sb, preferred_element_type=F32)
            vb = v_new.astype(BF16)
            o = (jnp.dot((qc * eg).astype(BF16), sb, preferred_element_type=F32)
                 + jnp.dot((qk * dec).astype(BF16), vb, preferred_element_type=F32))
            g_last = gcol[c_len - 1:c_len, :]
            kdec = (kc * jnp.exp(g_last - gcol)).astype(BF16)
            states[i] = states[i] * jnp.exp(g_last) + pl.dot(kdec, vb, trans_a=True)
            zc = z_ref[rows, i * hd:(i + 1) * hd].astype(F32)
            on = o * lax.rsqrt(jnp.mean(o * o, axis=-1, keepdims=True) + NORM_EPS) * nw * (zc * _sigmoid(zc))
            o_ref[rows, i * hd:(i + 1) * hd] = on.astype(BF16)
    for i in range(rep):
        s_ref[i] = states[i]


def _delta(proj, gcol, grow, conv_w, norm_w, batch, seq, k_heads, v_heads):
    m = proj.shape[0]
    hd = norm_w.shape[0]
    rep = v_heads // k_heads
    assert v_heads == rep * k_heads and 8 % rep == 0 and (2 * v_heads) % LANES == 0
    tb = _tile(seq, 256, DN_CHUNK)
    nt = seq // tb
    vblk = rep * hd
    k0 = k_heads
    v0 = 2 * k_heads // rep
    z0 = v0 + k_heads
    tok = lambda b, j, n: b * nt + n
    return pl.pallas_call(
        functools.partial(_delta_kernel, heads=v_heads, rep=rep),
        grid=(batch, k_heads, nt),
        in_specs=[pl.BlockSpec((tb, hd), lambda b, j, n: (tok(b, j, n), j)),
                  pl.BlockSpec((tb, hd), lambda b, j, n: (tok(b, j, n), k0 + j)),
                  pl.BlockSpec((tb, vblk), lambda b, j, n: (tok(b, j, n), v0 + j)),
                  pl.BlockSpec((tb, vblk), lambda b, j, n: (tok(b, j, n), z0 + j)),
                  pl.BlockSpec((tb, 2 * v_heads), lambda b, j, n: (tok(b, j, n), 0)),
                  pl.BlockSpec((8, tb), lambda b, j, n: ((rep * j) // 8, tok(b, j, n))),
                  pl.BlockSpec((DN_CONV, hd), lambda b, j, n: (0, j)),
                  pl.BlockSpec((DN_CONV, hd), lambda b, j, n: (0, k0 + j)),
                  pl.BlockSpec((DN_CONV, vblk), lambda b, j, n: (0, v0 + j)),
                  pl.BlockSpec((1, hd), lambda b, j, n: (0, 0))],
        out_specs=pl.BlockSpec((tb, vblk), lambda b, j, n: (tok(b, j, n), j)),
        out_shape=jax.ShapeDtypeStruct((m, v_heads * hd), BF16),
        scratch_shapes=[pltpu.VMEM((tb + CONV_HALO, (2 + rep) * hd), F32),
                        pltpu.VMEM((rep, hd, hd), F32)],
        compiler_params=_params(("parallel", "parallel", "arbitrary")),
        name="delta_rule",
    )(proj, proj, proj, proj, gcol, grow, conv_w, conv_w, conv_w, norm_w.reshape(1, hd))


def kernel(x, norm_w, sg_w_in, sg_ln_w, sg_ln_b, sg_w_s, sg_b_s, sg_w_out, dn_w_in, dn_conv_w, dn_a_log,
           dn_dt_bias, dn_norm_w, dn_w_out, mlp_w_up, mlp_w_down):
    batch, seq, d = x.shape
    m = batch * seq
    depth = norm_w.shape[0]
    h = x.reshape(m, d)
    t = _rms_cast(h, norm_w[0, 0])
    for i in range(depth):
        j = i // 2
        if i % 2 == 0:
            z = _mm_act(t, sg_w_in[j].astype(BF16), "gelu", "sg_in_gelu")
            mix = _sgu(z, sg_ln_w[j], sg_ln_b[j], sg_w_s[j], sg_b_s[j])
            w_out = sg_w_out[j]
        else:
            v_heads = dn_a_log.shape[1]
            hd = dn_norm_w.shape[1]
            v_dim = dn_w_out.shape[1]
            conv_ch = dn_conv_w.shape[2]
            k_heads = (conv_ch - v_dim) // (2 * hd)
            n_main = conv_ch + v_dim
            proj = _mm_act(t, dn_w_in[j][:, :n_main].astype(BF16), None, "dn_in")
            gcol, grow = _gates(t, dn_w_in[j][:, n_main:], dn_a_log[j], dn_dt_bias[j])
            mix = _delta(proj, gcol, grow, dn_conv_w[j], dn_norm_w[j], batch, seq, k_heads, v_heads)
            w_out = dn_w_out[j]
        h, t = _mm_resnorm(mix, w_out.astype(BF16), h, norm_w[i, 1], norm_w[i, 2], "mix_out")
        hid = _mm_act(t, mlp_w_up[i].astype(BF16), "relu2", "mlp_up")
        w_next = norm_w[i + 1, 0] if i + 1 < depth else None
        h, t = _mm_resnorm(hid, mlp_w_down[i].astype(BF16), h, norm_w[i, 3], w_next, "mlp_down")
    return h.reshape(batch, seq, d)
```

```python
import functools
import math

import jax
import jax.numpy as jnp
from jax import lax
from jax.experimental import pallas as pl
from jax.experimental.pallas import tpu as pltpu

NORM_EPS = 1e-6
LN_EPS = 1e-5
L2_EPS = 1e-6
SG_CHUNK = 128
DN_CHUNK = 64
DN_CONV = 4
CONV_HALO = 8
SUBLANES = 8
LANES = 128
V7X_VMEM_LIMIT_BYTES = 60000 * 1024

F32 = jnp.float32
BF16 = jnp.bfloat16


def _tile(dim, pref, align=LANES):
    if dim <= pref:
        return dim
    t = (pref // align) * align
    while t > align and dim % t:
        t -= align
    assert dim % t == 0, (dim, pref)
    return t


def _params(sem):
    return pltpu.CompilerParams(dimension_semantics=sem, vmem_limit_bytes=V7X_VMEM_LIMIT_BYTES)


def _sigmoid(x):
    return 1.0 / (1.0 + jnp.exp(-x))


def _softplus(x):
    return jnp.maximum(x, 0.0) + jnp.log1p(jnp.exp(-jnp.abs(x)))


def _dot(a, b):
    return jnp.dot(a, b, preferred_element_type=F32)


def _rms_cast_kernel(x_ref, w_ref, o_ref):
    x = x_ref[...]
    ms = jnp.mean(x * x, axis=-1, keepdims=True)
    o_ref[...] = (x * lax.rsqrt(ms + NORM_EPS) * w_ref[...]).astype(o_ref.dtype)


def _rms_cast(x, w):
    m, d = x.shape
    tm = _tile(m, 256, 8)
    return pl.pallas_call(
        _rms_cast_kernel,
        grid=(m // tm,),
        in_specs=[pl.BlockSpec((tm, d), lambda i: (i, 0)),
                  pl.BlockSpec((1, d), lambda i: (0, 0))],
        out_specs=pl.BlockSpec((tm, d), lambda i: (i, 0)),
        out_shape=jax.ShapeDtypeStruct((m, d), BF16),
        compiler_params=_params(("parallel",)),
        name="rms_cast",
    )(x, w.reshape(1, d))


def _mm_act_kernel(x_ref, w_ref, o_ref, *, act):
    acc = _dot(x_ref[...], w_ref[...])
    if act == "gelu":
        acc = 0.5 * acc * (1.0 + lax.erf(acc * math.sqrt(0.5)))
    elif act == "relu2":
        r = jnp.maximum(acc, 0.0)
        acc = r * r
    o_ref[...] = acc.astype(o_ref.dtype)


def _mm_act(x, w, act, name):
    m, k = x.shape
    n = w.shape[1]
    tm = _tile(m, 1024)
    tn = _tile(n, 1024)
    return pl.pallas_call(
        functools.partial(_mm_act_kernel, act=act),
        grid=(m // tm, n // tn),
        in_specs=[pl.BlockSpec((tm, k), lambda i, j: (i, 0)),
                  pl.BlockSpec((k, tn), lambda i, j: (0, j))],
        out_specs=pl.BlockSpec((tm, tn), lambda i, j: (i, j)),
        out_shape=jax.ShapeDtypeStruct((m, n), BF16),
        compiler_params=_params(("parallel", "arbitrary")),
        name=name,
    )(x, w)


def _mm_resnorm_kernel(x_ref, w_ref, res_ref, wpost_ref, wnext_ref, h_ref, *t_ref, nk, col_chunk, row_chunk):
    kk = pl.program_id(1)
    tm, n = h_ref.shape

    @pl.when(kk == 0)
    def _():
        h_ref[...] = jnp.zeros_like(h_ref)

    x = x_ref[...]
    for c in range(n // col_chunk):
        sl = slice(c * col_chunk, (c + 1) * col_chunk)
        h_ref[:, sl] += _dot(x, w_ref[:, sl])

    @pl.when(kk == nk - 1)
    def _():
        def body(r, carry):
            rows = pl.ds(pl.multiple_of(r * row_chunk, row_chunk), row_chunk)
            mval = h_ref[rows, :]
            ms = jnp.mean(mval * mval, axis=-1, keepdims=True)
            h = res_ref[rows, :] + mval * lax.rsqrt(ms + NORM_EPS) * wpost_ref[...]
            h_ref[rows, :] = h
            if t_ref:
                hs = jnp.mean(h * h, axis=-1, keepdims=True)
                t_ref[0][rows, :] = (h * lax.rsqrt(hs + NORM_EPS) * wnext_ref[...]).astype(BF16)
            return carry

        lax.fori_loop(0, tm // row_chunk, body, 0)


def _mm_resnorm(x, w, res, w_post, w_next, name):
    m, k = x.shape
    n = w.shape[1]
    tm = _tile(m, 512)
    tk = _tile(k, 512)
    nk = k // tk
    emit_t = w_next is not None
    wn = (w_next if emit_t else w_post).reshape(1, n)
    out_shape = [jax.ShapeDtypeStruct((m, n), F32)]
    out_specs = [pl.BlockSpec((tm, n), lambda i, j: (i, 0))]
    if emit_t:
        out_shape.append(jax.ShapeDtypeStruct((m, n), BF16))
        out_specs.append(pl.BlockSpec((tm, n), lambda i, j: (i, 0)))
    outs = pl.pallas_call(
        functools.partial(_mm_resnorm_kernel, nk=nk, col_chunk=_tile(n, 1024), row_chunk=_tile(tm, 32, 8)),
        grid=(m // tm, nk),
        in_specs=[pl.BlockSpec((tm, tk), lambda i, j: (i, j)),
                  pl.BlockSpec((tk, n), lambda i, j: (j, 0)),
                  pl.BlockSpec((tm, n), lambda i, j: (i, 0)),
                  pl.BlockSpec((1, n), lambda i, j: (0, 0)),
                  pl.BlockSpec((1, n), lambda i, j: (0, 0))],
        out_specs=out_specs,
        out_shape=out_shape,
        compiler_params=_params(("parallel", "arbitrary")),
        name=name,
    )(x, w, res, w_post.reshape(1, n), wn)
    return (outs[0], outs[1]) if emit_t else (outs[0], None)


def _sgu_kernel(u_ref, v_ref, lnw_ref, lnb_ref, ws_ref, bs_ref, o_ref, vn_ref, *, groups, row_chunk):
    tm, e = v_ref.shape
    gd = e // groups

    def ln_body(r, carry):
        rows = pl.ds(pl.multiple_of(r * row_chunk, row_chunk), row_chunk)
        v = v_ref[rows, :].astype(F32)
        mu = jnp.mean(v, axis=-1, keepdims=True)
        xc = v - mu
        var = jnp.mean(xc * xc, axis=-1, keepdims=True)
        vn_ref[rows, :] = (xc * lax.rsqrt(var + LN_EPS) * lnw_ref[...] + lnb_ref[...]).astype(BF16)
        return carry

    lax.fori_loop(0, tm // row_chunk, ln_body, 0)

    ri = lax.broadcasted_iota(jnp.int32, (SG_CHUNK, SG_CHUNK), 0)
    ci = lax.broadcasted_iota(jnp.int32, (SG_CHUNK, SG_CHUNK), 1)
    causal = ri >= ci

    def g_body(g, carry):
        w = jnp.where(causal, ws_ref[g], jnp.zeros((), BF16))
        bias = bs_ref[g]
        cols = pl.ds(pl.multiple_of(g * gd, gd), gd)
        for c in range(tm // SG_CHUNK):
            rows = slice(c * SG_CHUNK, (c + 1) * SG_CHUNK)
            s = _dot(w, vn_ref[rows, cols]) + bias
            o_ref[rows, cols] = (u_ref[rows, cols].astype(F32) * s).astype(BF16)
        return carry

    lax.fori_loop(0, groups, g_body, 0)


def _sgu(z, ln_w, ln_b, w_s, b_s):
    m, e2 = z.shape
    e = e2 // 2
    groups, c, _ = w_s.shape
    assert c == SG_CHUNK
    tm = _tile(m, 256, SG_CHUNK)
    return pl.pallas_call(
        functools.partial(_sgu_kernel, groups=groups, row_chunk=_tile(tm, 32, 8)),
        grid=(m // tm,),
        in_specs=[pl.BlockSpec((tm, e), lambda i: (i, 0)),
                  pl.BlockSpec((tm, e), lambda i: (i, 1)),
                  pl.BlockSpec((1, e), lambda i: (0, 0)),
                  pl.BlockSpec((1, e), lambda i: (0, 0)),
                  pl.BlockSpec((groups, c, c), lambda i: (0, 0, 0)),
                  pl.BlockSpec((groups, c, 1), lambda i: (0, 0, 0))],
        out_specs=pl.BlockSpec((tm, e), lambda i: (i, 0)),
        out_shape=jax.ShapeDtypeStruct((m, e), BF16),
        scratch_shapes=[pltpu.VMEM((tm, e), BF16)],
        compiler_params=_params(("parallel",)),
        name="sgu",
    )(z, z, ln_w.reshape(1, e), ln_b.reshape(1, e), w_s.astype(BF16), b_s.reshape(groups, c, 1))


def _chunk_cumsum(x, axis):
    pos = lax.broadcasted_iota(jnp.int32, x.shape, axis) % DN_CHUNK
    sh = 1
    while sh < DN_CHUNK:
        x = x + jnp.where(pos >= sh, pltpu.roll(x, sh, axis), 0.0)
        sh *= 2
    return x


def _gates_kernel(t_ref, wc_ref, wr_ref, ac_ref, dc_ref, ar_ref, dr_ref, gcol_ref, grow_ref, *, heads):
    t = t_ref[...]
    ba = _dot(t, wc_ref[...])
    beta = _sigmoid(ba)
    g = -jnp.exp(ac_ref[...]) * _softplus(ba + dc_ref[...])
    lane = lax.broadcasted_iota(jnp.int32, ba.shape, 1)
    gcol_ref[...] = jnp.where(lane < heads, beta, _chunk_cumsum(g, 0))
    ar = lax.dot_general(wr_ref[...], t, (((1,), (1,)), ((), ())), preferred_element_type=F32)
    gr = -jnp.exp(ar_ref[...]) * _softplus(ar + dr_ref[...])
    grow_ref[...] = _chunk_cumsum(gr, 1)


def _gates(t, w_ba, a_log, dt_bias):
    m, d = t.shape
    heads = a_log.shape[0]
    tm = _tile(m, 512)
    zeros = jnp.zeros((heads,), F32)
    ac = jnp.concatenate([zeros, a_log]).reshape(1, 2 * heads)
    dc = jnp.concatenate([zeros, dt_bias]).reshape(1, 2 * heads)
    wc = w_ba.astype(BF16)
    wr = w_ba[:, heads:].T.astype(BF16)
    return pl.pallas_call(
        functools.partial(_gates_kernel, heads=heads),
        grid=(m // tm,),
        in_specs=[pl.BlockSpec((tm, d), lambda i: (i, 0)),
                  pl.BlockSpec((d, 2 * heads), lambda i: (0, 0)),
                  pl.BlockSpec((heads, d), lambda i: (0, 0)),
                  pl.BlockSpec((1, 2 * heads), lambda i: (0, 0)),
                  pl.BlockSpec((1, 2 * heads), lambda i: (0, 0)),
                  pl.BlockSpec((heads, 1), lambda i: (0, 0)),
                  pl.BlockSpec((heads, 1), lambda i: (0, 0))],
        out_specs=[pl.BlockSpec((tm, 2 * heads), lambda i: (i, 0)),
                   pl.BlockSpec((heads, tm), lambda i: (0, i))],
        out_shape=[jax.ShapeDtypeStruct((m, 2 * heads), F32),
                   jax.ShapeDtypeStruct((heads, m), F32)],
        compiler_params=_params(("parallel",)),
        name="dn_gates",
    )(t, wc, wr, ac, dc, a_log.reshape(heads, 1), dt_bias.reshape(heads, 1))


def _delta_kernel(q_ref, k_ref, v_ref, z_ref, gcol_ref, grow_ref, cwq_ref, cwk_ref, cwv_ref, nw_ref,
                  o_ref, xpad_ref, s_ref, *, heads, rep, kh):
    j = pl.program_id(1)
    n = pl.program_id(2)
    tb = q_ref.shape[0]
    hd = nw_ref.shape[1]
    hs = kh * rep
    qw = kh * hd
    width = 2 * qw + hs * hd
    c_len = DN_CHUNK
    nc = tb // c_len

    @pl.when(n == 0)
    def _():
        xpad_ref[0:CONV_HALO, :] = jnp.zeros((CONV_HALO, width), F32)
        s_ref[...] = jnp.zeros_like(s_ref)

    @pl.when(n > 0)
    def _():
        xpad_ref[0:CONV_HALO, :] = xpad_ref[tb:tb + CONV_HALO, :]

    xpad_ref[CONV_HALO:, 0:qw] = q_ref[...].astype(F32)
    xpad_ref[CONV_HALO:, qw:2 * qw] = k_ref[...].astype(F32)
    xpad_ref[CONV_HALO:, 2 * qw:] = v_ref[...].astype(F32)

    cw = jnp.concatenate([cwq_ref[...], cwk_ref[...], cwv_ref[...]], axis=1)
    y = jnp.zeros((tb, width), F32)
    for i in range(DN_CONV):
        y = y + cw[i:i + 1, :] * xpad_ref[pl.ds(CONV_HALO - (DN_CONV - 1) + i, tb), :]
    y = y * _sigmoid(y)

    qn, kn = [], []
    for a in range(kh):
        qa = y[:, a * hd:(a + 1) * hd]
        ka = y[:, qw + a * hd:qw + (a + 1) * hd]
        qn.append(qa * lax.rsqrt(jnp.sum(qa * qa, axis=-1, keepdims=True) + L2_EPS) * (hd ** -0.5))
        kn.append(ka * lax.rsqrt(jnp.sum(ka * ka, axis=-1, keepdims=True) + L2_EPS))

    gsel = pltpu.roll(gcol_ref[...], (2 * heads - hs * j) % (2 * heads), 1)
    grow = grow_ref[...]

    ri = lax.broadcasted_iota(jnp.int32, (c_len, c_len), 0)
    ci = lax.broadcasted_iota(jnp.int32, (c_len, c_len), 1)
    tril = ri >= ci
    strict = ri > ci
    eye = (ri == ci).astype(F32)
    lvl_masks = []
    s = 1
    while s < c_len:
        lvl_masks.append(((ri // (2 * s)) == (ci // (2 * s))) & ((ri // s) != (ci // s)) & strict)
        s *= 2
    nw = nw_ref[...]

    pairs = [(a, i, c) for c in range(nc) for a in range(kh) for i in range(rep)]
    rows = [slice(c * c_len, (c + 1) * c_len) for c in range(nc)]

    qkk = {}
    for c in range(nc):
        for a in range(kh):
            kb = kn[a][rows[c]].astype(BF16)
            lhs = jnp.concatenate([qn[a][rows[c]].astype(BF16), kb], axis=0)
            qkk[a, c] = pl.dot(lhs, kb, trans_b=True)

    bcol, gcol, dec, amat, dmat = {}, {}, {}, {}, {}
    for p in pairs:
        a, i, c = p
        hl = a * rep + i
        bcol[p] = gsel[rows[c], hl:hl + 1]
        gcol[p] = gsel[rows[c], heads + hl:heads + hl + 1]
        gr = grow[hl:hl + 1, rows[c]]
        dec[p] = jnp.where(tril, jnp.exp(jnp.where(tril, gcol[p] - gr, 0.0)), 0.0)
        amat[p] = jnp.where(strict, bcol[p] * qkk[a, c][c_len:] * dec[p], 0.0)
        dmat[p] = eye - jnp.where(lvl_masks[0], amat[p], 0.0)

    for mask in lvl_masks[1:]:
        db = {p: dmat[p].astype(BF16) for p in pairs}
        x = {p: _dot(db[p], jnp.where(mask, amat[p], 0.0).astype(BF16)) for p in pairs}
        for p in pairs:
            dmat[p] = dmat[p] - _dot(x[p].astype(BF16), db[p])

    eg, rhs = {}, {}
    for p in pairs:
        a, i, c = p
        eg[p] = jnp.exp(gcol[p])
        vc = y[rows[c], 2 * qw + (a * rep + i) * hd:2 * qw + (a * rep + i + 1) * hd]
        rhs[p] = jnp.concatenate([bcol[p] * vc, (bcol[p] * eg[p]) * kn[a][rows[c]]], axis=1)
    sol = {p: (rhs[p] + _dot((dmat[p] - eye).astype(BF16), rhs[p].astype(BF16))).astype(BF16) for p in pairs}
    att = {p: _dot((qkk[p[0], p[2]][:c_len] * dec[p]).astype(BF16), sol[p]) for p in pairs}
    glast, ks = {}, {}
    for p in pairs:
        a, i, c = p
        glast[p] = gcol[p][c_len - 1:c_len, :]
        kd = (kn[a][rows[c]] * jnp.exp(glast[p] - gcol[p])).astype(BF16)
        ks[p] = pl.dot(kd, sol[p], trans_a=True)
    lhs2 = {}
    for p in pairs:
        a, i, c = p
        qeff = qn[a][rows[c]] * eg[p] - att[p][:, hd:]
        lhs2[p] = jnp.concatenate([qeff.astype(BF16), ks[p][:, hd:].astype(BF16)], axis=0)

    states = [s_ref[hl] for hl in range(hs)]
    for c in range(nc):
        r = {}
        for a in range(kh):
            for i in range(rep):
                hl = a * rep + i
                r[hl] = _dot(lhs2[a, i, c], states[hl].astype(BF16))
        for a in range(kh):
            for i in range(rep):
                hl = a * rep + i
                p = (a, i, c)
                o = att[p][:, :hd] + r[hl][:c_len]
                states[hl] = states[hl] * jnp.exp(glast[p]) - r[hl][c_len:] + ks[p][:, :hd]
                zc = z_ref[rows[c], hl * hd:(hl + 1) * hd].astype(F32)
                on = o * lax.rsqrt(jnp.mean(o * o, axis=-1, keepdims=True) + NORM_EPS) * nw * (zc * _sigmoid(zc))
                o_ref[rows[c], hl * hd:(hl + 1) * hd] = on.astype(BF16)
    for hl in range(hs):
        s_ref[hl] = states[hl]


def _delta(proj, gcol, grow, conv_w, norm_w, batch, seq, k_heads, v_heads):
    m = proj.shape[0]
    hd = norm_w.shape[0]
    rep = v_heads // k_heads
    hs = SUBLANES
    kh = hs // rep
    assert v_heads == rep * k_heads and hs == kh * rep and k_heads % kh == 0 and 2 * v_heads == LANES
    tb = _tile(seq, 256, DN_CHUNK)
    nt = seq // tb
    qblk, vblk = kh * hd, hs * hd
    k0 = k_heads // kh
    v0 = 2 * k_heads // hs
    z0 = v0 + v_heads // hs
    tok = lambda b, j, n: b * nt + n
    return pl.pallas_call(
        functools.partial(_delta_kernel, heads=v_heads, rep=rep, kh=kh),
        grid=(batch, k_heads // kh, nt),
        in_specs=[pl.BlockSpec((tb, qblk), lambda b, j, n: (tok(b, j, n), j)),
                  pl.BlockSpec((tb, qblk), lambda b, j, n: (tok(b, j, n), k0 + j)),
                  pl.BlockSpec((tb, vblk), lambda b, j, n: (tok(b, j, n), v0 + j)),
                  pl.BlockSpec((tb, vblk), lambda b, j, n: (tok(b, j, n), z0 + j)),
                  pl.BlockSpec((tb, 2 * v_heads), lambda b, j, n: (tok(b, j, n), 0)),
                  pl.BlockSpec((hs, tb), lambda b, j, n: (j, tok(b, j, n))),
                  pl.BlockSpec((DN_CONV, qblk), lambda b, j, n: (0, j)),
                  pl.BlockSpec((DN_CONV, qblk), lambda b, j, n: (0, k0 + j)),
                  pl.BlockSpec((DN_CONV, vblk), lambda b, j, n: (0, v0 + j)),
                  pl.BlockSpec((1, hd), lambda b, j, n: (0, 0))],
        out_specs=pl.BlockSpec((tb, vblk), lambda b, j, n: (tok(b, j, n), j)),
        out_shape=jax.ShapeDtypeStruct((m, v_heads * hd), BF16),
        scratch_shapes=[pltpu.VMEM((tb + CONV_HALO, 2 * qblk + vblk), F32),
                        pltpu.VMEM((hs, hd, hd), F32)],
        compiler_params=_params(("parallel", "parallel", "arbitrary")),
        name="delta_rule",
    )(proj, proj, proj, proj, gcol, grow, conv_w, conv_w, conv_w, norm_w.reshape(1, hd))


def kernel(x, norm_w, sg_w_in, sg_ln_w, sg_ln_b, sg_w_s, sg_b_s, sg_w_out, dn_w_in, dn_conv_w, dn_a_log,
           dn_dt_bias, dn_norm_w, dn_w_out, mlp_w_up, mlp_w_down):
    batch, seq, d = x.shape
    m = batch * seq
    depth = norm_w.shape[0]
    h = x.reshape(m, d)
    t = _rms_cast(h, norm_w[0, 0])
    for i in range(depth):
        j = i // 2
        if i % 2 == 0:
            z = _mm_act(t, sg_w_in[j].astype(BF16), "gelu", "sg_in_gelu")
            mix = _sgu(z, sg_ln_w[j], sg_ln_b[j], sg_w_s[j], sg_b_s[j])
            w_out = sg_w_out[j]
        else:
            v_heads = dn_a_log.shape[1]
            hd = dn_norm_w.shape[1]
            v_dim = dn_w_out.shape[1]
            conv_ch = dn_conv_w.shape[2]
            k_heads = (conv_ch - v_dim) // (2 * hd)
            n_main = conv_ch + v_dim
            proj = _mm_act(t, dn_w_in[j][:, :n_main].astype(BF16), None, "dn_in")
            gcol, grow = _gates(t, dn_w_in[j][:, n_main:], dn_a_log[j], dn_dt_bias[j])
            mix = _delta(proj, gcol, grow, dn_conv_w[j], dn_norm_w[j], batch, seq, k_heads, v_heads)
            w_out = dn_w_out[j]
        h, t = _mm_resnorm(mix, w_out.astype(BF16), h, norm_w[i, 1], norm_w[i, 2], "mix_out")
        hid = _mm_act(t, mlp_w_up[i].astype(BF16), "relu2", "mlp_up")
        w_next = norm_w[i + 1, 0] if i + 1 < depth else None
        h, t = _mm_resnorm(hid, mlp_w_down[i].astype(BF16), h, norm_w[i, 3], w_next, "mlp_down")
    return h.reshape(batch, seq, d)
```

```python
import functools
import math

import jax
import jax.numpy as jnp
from jax import lax
from jax.experimental import pallas as pl
from jax.experimental.pallas import tpu as pltpu

NORM_EPS = 1e-6
LN_EPS = 1e-5
L2_EPS = 1e-6
SG_CHUNK = 128
DN_CHUNK = 64
DN_CONV = 4
CONV_HALO = 8
SUBLANES = 8
LANES = 128
V7X_VMEM_LIMIT_BYTES = 60000 * 1024

F32 = jnp.float32
BF16 = jnp.bfloat16


def _tile(dim, pref, align=LANES):
    if dim <= pref:
        return dim
    t = (pref // align) * align
    while t > align and dim % t:
        t -= align
    assert dim % t == 0, (dim, pref)
    return t


def _params(sem):
    return pltpu.CompilerParams(dimension_semantics=sem, vmem_limit_bytes=V7X_VMEM_LIMIT_BYTES)


def _sigmoid(x):
    return 1.0 / (1.0 + jnp.exp(-x))


def _softplus(x):
    return jnp.maximum(x, 0.0) + jnp.log1p(jnp.exp(-jnp.abs(x)))


def _dot(a, b):
    return jnp.dot(a, b, preferred_element_type=F32)


def _rms_cast_kernel(x_ref, w_ref, o_ref):
    x = x_ref[...]
    ms = jnp.mean(x * x, axis=-1, keepdims=True)
    o_ref[...] = (x * lax.rsqrt(ms + NORM_EPS) * w_ref[...]).astype(o_ref.dtype)


def _rms_cast(x, w):
    m, d = x.shape
    tm = _tile(m, 256, 8)
    return pl.pallas_call(
        _rms_cast_kernel,
        grid=(m // tm,),
        in_specs=[pl.BlockSpec((tm, d), lambda i: (i, 0)),
                  pl.BlockSpec((1, d), lambda i: (0, 0))],
        out_specs=pl.BlockSpec((tm, d), lambda i: (i, 0)),
        out_shape=jax.ShapeDtypeStruct((m, d), BF16),
        compiler_params=_params(("parallel",)),
        name="rms_cast",
    )(x, w.reshape(1, d))


def _mm_act_kernel(x_ref, w_ref, o_ref, *, act):
    acc = _dot(x_ref[...], w_ref[...])
    if act == "gelu":
        acc = 0.5 * acc * (1.0 + lax.erf(acc * math.sqrt(0.5)))
    elif act == "relu2":
        r = jnp.maximum(acc, 0.0)
        acc = r * r
    o_ref[...] = acc.astype(o_ref.dtype)


def _mm_act(x, w, layer, n, act, name):
    m, k = x.shape
    tm = _tile(m, 1024)
    tn = _tile(n, 1024)
    return pl.pallas_call(
        functools.partial(_mm_act_kernel, act=act),
        grid=(m // tm, n // tn),
        in_specs=[pl.BlockSpec((tm, k), lambda i, j: (i, 0)),
                  pl.BlockSpec((None, k, tn), lambda i, j: (layer, 0, j))],
        out_specs=pl.BlockSpec((tm, tn), lambda i, j: (i, j)),
        out_shape=jax.ShapeDtypeStruct((m, n), BF16),
        compiler_params=_params(("parallel", "arbitrary")),
        name=name,
    )(x, w)


def _mm_resnorm_kernel(x_ref, w_ref, res_ref, wpost_ref, wnext_ref, h_ref, *t_ref, nk, col_chunk, row_chunk):
    kk = pl.program_id(1)
    tm, n = h_ref.shape

    @pl.when(kk == 0)
    def _():
        h_ref[...] = jnp.zeros_like(h_ref)

    x = x_ref[...]
    for c in range(n // col_chunk):
        sl = slice(c * col_chunk, (c + 1) * col_chunk)
        h_ref[:, sl] += _dot(x, w_ref[:, sl])

    @pl.when(kk == nk - 1)
    def _():
        def body(r, carry):
            rows = pl.ds(pl.multiple_of(r * row_chunk, row_chunk), row_chunk)
            mval = h_ref[rows, :]
            ms = jnp.mean(mval * mval, axis=-1, keepdims=True)
            h = res_ref[rows, :] + mval * lax.rsqrt(ms + NORM_EPS) * wpost_ref[...]
            h_ref[rows, :] = h
            if t_ref:
                hs = jnp.mean(h * h, axis=-1, keepdims=True)
                t_ref[0][rows, :] = (h * lax.rsqrt(hs + NORM_EPS) * wnext_ref[...]).astype(BF16)
            return carry

        lax.fori_loop(0, tm // row_chunk, body, 0)


def _mm_resnorm(x, w, layer, res, w_post, w_next, name):
    m, k = x.shape
    n = w.shape[2]
    tm = _tile(m, 512)
    tk = _tile(k, 1024)
    nk = k // tk
    emit_t = w_next is not None
    wn = (w_next if emit_t else w_post).reshape(1, n)
    out_shape = [jax.ShapeDtypeStruct((m, n), F32)]
    out_specs = [pl.BlockSpec((tm, n), lambda i, j: (i, 0))]
    if emit_t:
        out_shape.append(jax.ShapeDtypeStruct((m, n), BF16))
        out_specs.append(pl.BlockSpec((tm, n), lambda i, j: (i, 0)))
    outs = pl.pallas_call(
        functools.partial(_mm_resnorm_kernel, nk=nk, col_chunk=_tile(n, 1024), row_chunk=_tile(tm, 32, 8)),
        grid=(m // tm, nk),
        in_specs=[pl.BlockSpec((tm, tk), lambda i, j: (i, j)),
                  pl.BlockSpec((None, tk, n), lambda i, j: (layer, j, 0)),
                  pl.BlockSpec((tm, n), lambda i, j: (i, 0), pipeline_mode=pl.Buffered(1)),
                  pl.BlockSpec((1, n), lambda i, j: (0, 0)),
                  pl.BlockSpec((1, n), lambda i, j: (0, 0))],
        out_specs=out_specs,
        out_shape=out_shape,
        compiler_params=_params(("parallel", "arbitrary")),
        name=name,
    )(x, w, res, w_post.reshape(1, n), wn)
    return (outs[0], outs[1]) if emit_t else (outs[0], None)


def _sgu_kernel(u_ref, v_ref, lnw_ref, lnb_ref, ws_ref, bs_ref, o_ref, vn_ref, *, groups, row_chunk):
    tm, e = v_ref.shape
    gd = e // groups

    def ln_body(r, carry):
        rows = pl.ds(pl.multiple_of(r * row_chunk, row_chunk), row_chunk)
        v = v_ref[rows, :].astype(F32)
        mu = jnp.mean(v, axis=-1, keepdims=True)
        xc = v - mu
        var = jnp.mean(xc * xc, axis=-1, keepdims=True)
        vn_ref[rows, :] = (xc * lax.rsqrt(var + LN_EPS) * lnw_ref[...] + lnb_ref[...]).astype(BF16)
        return carry

    lax.fori_loop(0, tm // row_chunk, ln_body, 0, unroll=2)

    ri = lax.broadcasted_iota(jnp.int32, (SG_CHUNK, SG_CHUNK), 0)
    ci = lax.broadcasted_iota(jnp.int32, (SG_CHUNK, SG_CHUNK), 1)
    causal = ri >= ci

    for g in range(groups):
        w = jnp.where(causal, ws_ref[g], jnp.zeros((), BF16))
        bias = bs_ref[g]
        cols = slice(g * gd, (g + 1) * gd)
        for c in range(tm // SG_CHUNK):
            rows = slice(c * SG_CHUNK, (c + 1) * SG_CHUNK)
            s = _dot(w, vn_ref[rows, cols]) + bias
            o_ref[rows, cols] = (u_ref[rows, cols].astype(F32) * s).astype(BF16)


def _sgu(z, ln_w, ln_b, w_s, b_s):
    m, e2 = z.shape
    e = e2 // 2
    groups, c, _ = w_s.shape
    assert c == SG_CHUNK
    tm = _tile(m, 256, SG_CHUNK)
    return pl.pallas_call(
        functools.partial(_sgu_kernel, groups=groups, row_chunk=_tile(tm, 32, 8)),
        grid=(m // tm,),
        in_specs=[pl.BlockSpec((tm, e), lambda i: (i, 0)),
                  pl.BlockSpec((tm, e), lambda i: (i, 1)),
                  pl.BlockSpec((1, e), lambda i: (0, 0)),
                  pl.BlockSpec((1, e), lambda i: (0, 0)),
                  pl.BlockSpec((groups, c, c), lambda i: (0, 0, 0)),
                  pl.BlockSpec((groups, c, 1), lambda i: (0, 0, 0))],
        out_specs=pl.BlockSpec((tm, e), lambda i: (i, 0)),
        out_shape=jax.ShapeDtypeStruct((m, e), BF16),
        scratch_shapes=[pltpu.VMEM((tm, e), BF16)],
        compiler_params=_params(("parallel",)),
        name="sgu",
    )(z, z, ln_w.reshape(1, e), ln_b.reshape(1, e), w_s.astype(BF16), b_s.reshape(groups, c, 1))


def _chunk_cumsum(x, axis):
    pos = lax.broadcasted_iota(jnp.int32, x.shape, axis) % DN_CHUNK
    sh = 1
    while sh < DN_CHUNK:
        x = x + jnp.where(pos >= sh, pltpu.roll(x, sh, axis), 0.0)
        sh *= 2
    return x


def _gates_kernel(t_ref, w_ref, ac_ref, dc_ref, gcol_ref, grow_ref, *, heads):
    ba = _dot(t_ref[...], w_ref[...])
    beta = _sigmoid(ba)
    g = -jnp.exp(ac_ref[...]) * _softplus(ba + dc_ref[...])
    gc = _chunk_cumsum(g, 0)
    lane = lax.broadcasted_iota(jnp.int32, ba.shape, 1)
    gcol_ref[...] = jnp.where(lane < heads, beta, gc)
    grow_ref[...] = gc.T[heads:, :]


def _gates(t, w, layer, col0, a_log, dt_bias):
    m, d = t.shape
    heads = a_log.shape[0]
    assert 2 * heads == LANES and col0 % LANES == 0
    tm = _tile(m, 512)
    zeros = jnp.zeros((heads,), F32)
    ac = jnp.concatenate([zeros, a_log]).reshape(1, 2 * heads)
    dc = jnp.concatenate([zeros, dt_bias]).reshape(1, 2 * heads)
    return pl.pallas_call(
        functools.partial(_gates_kernel, heads=heads),
        grid=(m // tm,),
        in_specs=[pl.BlockSpec((tm, d), lambda i: (i, 0)),
                  pl.BlockSpec((None, d, 2 * heads), lambda i: (layer, 0, col0 // LANES)),
                  pl.BlockSpec((1, 2 * heads), lambda i: (0, 0)),
                  pl.BlockSpec((1, 2 * heads), lambda i: (0, 0))],
        out_specs=[pl.BlockSpec((tm, 2 * heads), lambda i: (i, 0)),
                   pl.BlockSpec((heads, tm), lambda i: (0, i))],
        out_shape=[jax.ShapeDtypeStruct((m, 2 * heads), F32),
                   jax.ShapeDtypeStruct((heads, m), F32)],
        compiler_params=_params(("parallel",)),
        name="dn_gates",
    )(t, w, ac, dc)


def _delta_kernel(q_ref, k_ref, v_ref, z_ref, gcol_ref, grow_ref, cwq_ref, cwk_ref, cwv_ref, nw_ref,
                  o_ref, xpad_ref, s_ref, *, heads, rep, kh):
    j = pl.program_id(1)
    n = pl.program_id(2)
    tb = q_ref.shape[0]
    hd = nw_ref.shape[1]
    hs = kh * rep
    qw = kh * hd
    width = 2 * qw + hs * hd
    c_len = DN_CHUNK
    nc = tb // c_len

    @pl.when(n == 0)
    def _():
        xpad_ref[0:CONV_HALO, :] = jnp.zeros((CONV_HALO, width), F32)
        s_ref[...] = jnp.zeros_like(s_ref)

    @pl.when(n > 0)
    def _():
        xpad_ref[0:CONV_HALO, :] = xpad_ref[tb:tb + CONV_HALO, :]

    xpad_ref[CONV_HALO:, 0:qw] = q_ref[...].astype(F32)
    xpad_ref[CONV_HALO:, qw:2 * qw] = k_ref[...].astype(F32)
    xpad_ref[CONV_HALO:, 2 * qw:] = v_ref[...].astype(F32)

    cw = jnp.concatenate([cwq_ref[...], cwk_ref[...], cwv_ref[...]], axis=1)
    y = jnp.zeros((tb, width), F32)
    for i in range(DN_CONV):
        y = y + cw[i:i + 1, :] * xpad_ref[pl.ds(CONV_HALO - (DN_CONV - 1) + i, tb), :]
    y = y * _sigmoid(y)

    qn, kn = [], []
    for a in range(kh):
        qa = y[:, a * hd:(a + 1) * hd]
        ka = y[:, qw + a * hd:qw + (a + 1) * hd]
        qn.append(qa * lax.rsqrt(jnp.sum(qa * qa, axis=-1, keepdims=True) + L2_EPS) * (hd ** -0.5))
        kn.append(ka * lax.rsqrt(jnp.sum(ka * ka, axis=-1, keepdims=True) + L2_EPS))

    gsel = pltpu.roll(gcol_ref[...], (2 * heads - hs * j) % (2 * heads), 1)
    grow = grow_ref[...]

    ri = lax.broadcasted_iota(jnp.int32, (c_len, c_len), 0)
    ci = lax.broadcasted_iota(jnp.int32, (c_len, c_len), 1)
    tril = ri >= ci
    strict = ri > ci
    eye = (ri == ci).astype(F32)
    lvl_masks = []
    s = 1
    while s < c_len:
        lvl_masks.append(((ri // (2 * s)) == (ci // (2 * s))) & ((ri // s) != (ci // s)) & strict)
        s *= 2
    nw = nw_ref[...]

    pairs = [(a, i, c) for c in range(nc) for a in range(kh) for i in range(rep)]
    rows = [slice(c * c_len, (c + 1) * c_len) for c in range(nc)]

    qkk = {}
    for c in range(nc):
        for a in range(kh):
            kb = kn[a][rows[c]].astype(BF16)
            lhs = jnp.concatenate([qn[a][rows[c]].astype(BF16), kb], axis=0)
            qkk[a, c] = pl.dot(lhs, kb, trans_b=True)

    bcol, gcol, dec, amat, dmat = {}, {}, {}, {}, {}
    for p in pairs:
        a, i, c = p
        hl = a * rep + i
        bcol[p] = gsel[rows[c], hl:hl + 1]
        gcol[p] = gsel[rows[c], heads + hl:heads + hl + 1]
        gr = grow[hl:hl + 1, rows[c]]
        dec[p] = jnp.where(tril, jnp.exp(jnp.where(tril, gcol[p] - gr, 0.0)), 0.0)
        amat[p] = jnp.where(strict, bcol[p] * qkk[a, c][c_len:] * dec[p], 0.0)
        dmat[p] = eye - jnp.where(lvl_masks[0], amat[p], 0.0)

    for mask in lvl_masks[1:]:
        db = {p: dmat[p].astype(BF16) for p in pairs}
        x = {p: _dot(db[p], jnp.where(mask, amat[p], 0.0).astype(BF16)) for p in pairs}
        for p in pairs:
            dmat[p] = dmat[p] - _dot(x[p].astype(BF16), db[p])

    eg, rhs = {}, {}
    for p in pairs:
        a, i, c = p
        eg[p] = jnp.exp(gcol[p])
        vc = y[rows[c], 2 * qw + (a * rep + i) * hd:2 * qw + (a * rep + i + 1) * hd]
        rhs[p] = jnp.concatenate([bcol[p] * vc, (bcol[p] * eg[p]) * kn[a][rows[c]]], axis=1)
    sol = {p: (rhs[p] + _dot((dmat[p] - eye).astype(BF16), rhs[p].astype(BF16))).astype(BF16) for p in pairs}
    att = {p: _dot((qkk[p[0], p[2]][:c_len] * dec[p]).astype(BF16), sol[p]) for p in pairs}
    glast, ks = {}, {}
    for p in pairs:
        a, i, c = p
        glast[p] = gcol[p][c_len - 1:c_len, :]
        kd = (kn[a][rows[c]] * jnp.exp(glast[p] - gcol[p])).astype(BF16)
        ks[p] = pl.dot(kd, sol[p], trans_a=True)
    lhs2 = {}
    for p in pairs:
        a, i, c = p
        qeff = qn[a][rows[c]] * eg[p] - att[p][:, hd:]
        lhs2[p] = jnp.concatenate([qeff.astype(BF16), ks[p][:, hd:].astype(BF16)], axis=0)

    states = [s_ref[hl] for hl in range(hs)]
    for c in range(nc):
        r = {}
        for a in range(kh):
            for i in range(rep):
                hl = a * rep + i
                r[hl] = _dot(lhs2[a, i, c], states[hl].astype(BF16))
        for a in range(kh):
            for i in range(rep):
                hl = a * rep + i
                p = (a, i, c)
                o = att[p][:, :hd] + r[hl][:c_len]
                states[hl] = states[hl] * jnp.exp(glast[p]) - r[hl][c_len:] + ks[p][:, :hd]
                zc = z_ref[rows[c], hl * hd:(hl + 1) * hd].astype(F32)
                on = o * lax.rsqrt(jnp.mean(o * o, axis=-1, keepdims=True) + NORM_EPS) * nw * (zc * _sigmoid(zc))
                o_ref[rows[c], hl * hd:(hl + 1) * hd] = on.astype(BF16)
    for hl in range(hs):
        s_ref[hl] = states[hl]


def _delta(proj, gcol, grow, conv_w, norm_w, batch, seq, k_heads, v_heads):
    m = proj.shape[0]
    hd = norm_w.shape[0]
    rep = v_heads // k_heads
    hs = SUBLANES
    kh = hs // rep
    assert v_heads == rep * k_heads and hs == kh * rep and k_heads % kh == 0 and 2 * v_heads == LANES
    tb = _tile(seq, 256, DN_CHUNK)
    nt = seq // tb
    qblk, vblk = kh * hd, hs * hd
    k0 = k_heads // kh
    v0 = 2 * k_heads // hs
    z0 = v0 + v_heads // hs
    tok = lambda b, j, n: b * nt + n
    return pl.pallas_call(
        functools.partial(_delta_kernel, heads=v_heads, rep=rep, kh=kh),
        grid=(batch, k_heads // kh, nt),
        in_specs=[pl.BlockSpec((tb, qblk), lambda b, j, n: (tok(b, j, n), j)),
                  pl.BlockSpec((tb, qblk), lambda b, j, n: (tok(b, j, n), k0 + j)),
                  pl.BlockSpec((tb, vblk), lambda b, j, n: (tok(b, j, n), v0 + j)),
                  pl.BlockSpec((tb, vblk), lambda b, j, n: (tok(b, j, n), z0 + j)),
                  pl.BlockSpec((tb, 2 * v_heads), lambda b, j, n: (tok(b, j, n), 0)),
                  pl.BlockSpec((hs, tb), lambda b, j, n: (j, tok(b, j, n))),
                  pl.BlockSpec((DN_CONV, qblk), lambda b, j, n: (0, j)),
                  pl.BlockSpec((DN_CONV, qblk), lambda b, j, n: (0, k0 + j)),
                  pl.BlockSpec((DN_CONV, vblk), lambda b, j, n: (0, v0 + j)),
                  pl.BlockSpec((1, hd), lambda b, j, n: (0, 0))],
        out_specs=pl.BlockSpec((tb, vblk), lambda b, j, n: (tok(b, j, n), j)),
        out_shape=jax.ShapeDtypeStruct((m, v_heads * hd), BF16),
        scratch_shapes=[pltpu.VMEM((tb + CONV_HALO, 2 * qblk + vblk), F32),
                        pltpu.VMEM((hs, hd, hd), F32)],
        compiler_params=_params(("parallel", "parallel", "arbitrary")),
        name="delta_rule",
    )(proj, proj, proj, proj, gcol, grow, conv_w, conv_w, conv_w, norm_w.reshape(1, hd))


def kernel(x, norm_w, sg_w_in, sg_ln_w, sg_ln_b, sg_w_s, sg_b_s, sg_w_out, dn_w_in, dn_conv_w, dn_a_log,
           dn_dt_bias, dn_norm_w, dn_w_out, mlp_w_up, mlp_w_down):
    batch, seq, d = x.shape
    m = batch * seq
    depth = norm_w.shape[0]
    h = x.reshape(m, d)
    sg_w_in, sg_w_out, dn_w_in, dn_w_out, mlp_w_up, mlp_w_down = (
        w.astype(BF16) for w in (sg_w_in, sg_w_out, dn_w_in, dn_w_out, mlp_w_up, mlp_w_down))
    t = _rms_cast(h, norm_w[0, 0])
    for i in range(depth):
        j = i // 2
        if i % 2 == 0:
            z = _mm_act(t, sg_w_in, j, sg_w_in.shape[2], "gelu", "sg_in_gelu")
            mix = _sgu(z, sg_ln_w[j], sg_ln_b[j], sg_w_s[j], sg_b_s[j])
            w_out = sg_w_out
        else:
            v_heads = dn_a_log.shape[1]
            hd = dn_norm_w.shape[1]
            v_dim = dn_w_out.shape[1]
            conv_ch = dn_conv_w.shape[2]
            k_heads = (conv_ch - v_dim) // (2 * hd)
            n_main = conv_ch + v_dim
            proj = _mm_act(t, dn_w_in, j, n_main, None, "dn_in")
            gcol, grow = _gates(t, dn_w_in, j, n_main, dn_a_log[j], dn_dt_bias[j])
            mix = _delta(proj, gcol, grow, dn_conv_w[j], dn_norm_w[j], batch, seq, k_heads, v_heads)
            w_out = dn_w_out
        h, t = _mm_resnorm(mix, w_out, j, h, norm_w[i, 1], norm_w[i, 2], "mix_out")
        hid = _mm_act(t, mlp_w_up, i, mlp_w_up.shape[2], "relu2", "mlp_up")
        w_next = norm_w[i + 1, 0] if i + 1 < depth else None
        h, t = _mm_resnorm(hid, mlp_w_down, i, h, norm_w[i, 3], w_next, "mlp_down")
    return h.reshape(batch, seq, d)
```

```python
import functools
import math

import jax
import jax.numpy as jnp
from jax import lax
from jax.experimental import pallas as pl
from jax.experimental.pallas import tpu as pltpu

NORM_EPS = 1e-6
LN_EPS = 1e-5
L2_EPS = 1e-6
SG_CHUNK = 128
DN_CHUNK = 64
DN_CONV = 4
CONV_HALO = 8
SUBLANES = 8
LANES = 128
V7X_VMEM_LIMIT_BYTES = 60000 * 1024

F32 = jnp.float32
BF16 = jnp.bfloat16


def _tile(dim, pref, align=LANES):
    if dim <= pref:
        return dim
    t = (pref // align) * align
    while t > align and dim % t:
        t -= align
    assert dim % t == 0, (dim, pref)
    return t


def _params(sem):
    return pltpu.CompilerParams(dimension_semantics=sem, vmem_limit_bytes=V7X_VMEM_LIMIT_BYTES)


def _sigmoid(x):
    return 1.0 / (1.0 + jnp.exp(-x))


def _softplus(x):
    return jnp.maximum(x, 0.0) + jnp.log1p(jnp.exp(-jnp.abs(x)))


def _dot(a, b):
    return jnp.dot(a, b, preferred_element_type=F32)


def _rms_cast_kernel(x_ref, w_ref, o_ref):
    x = x_ref[...]
    ms = jnp.mean(x * x, axis=-1, keepdims=True)
    o_ref[...] = (x * lax.rsqrt(ms + NORM_EPS) * w_ref[...]).astype(o_ref.dtype)


def _rms_cast(x, w):
    m, d = x.shape
    tm = _tile(m, 256, 8)
    return pl.pallas_call(
        _rms_cast_kernel,
        grid=(m // tm,),
        in_specs=[pl.BlockSpec((tm, d), lambda i: (i, 0)),
                  pl.BlockSpec((1, d), lambda i: (0, 0))],
        out_specs=pl.BlockSpec((tm, d), lambda i: (i, 0)),
        out_shape=jax.ShapeDtypeStruct((m, d), BF16),
        compiler_params=_params(("parallel",)),
        name="rms_cast",
    )(x, w.reshape(1, d))


def _mm_act_kernel(x_ref, w_ref, o_ref, *, act):
    acc = _dot(x_ref[...], w_ref[...])
    if act == "gelu":
        acc = 0.5 * acc * (1.0 + lax.erf(acc * math.sqrt(0.5)))
    elif act == "relu2":
        r = jnp.maximum(acc, 0.0)
        acc = r * r
    o_ref[...] = acc.astype(o_ref.dtype)


def _mm_act(x, w, layer, n, act, name):
    m, k = x.shape
    tm = _tile(m, 1024)
    tn = _tile(n, 1024)
    return pl.pallas_call(
        functools.partial(_mm_act_kernel, act=act),
        grid=(m // tm, n // tn),
        in_specs=[pl.BlockSpec((tm, k), lambda i, j: (i, 0)),
                  pl.BlockSpec((None, k, tn), lambda i, j: (layer, 0, j))],
        out_specs=pl.BlockSpec((tm, tn), lambda i, j: (i, j)),
        out_shape=jax.ShapeDtypeStruct((m, n), BF16),
        compiler_params=_params(("parallel", "arbitrary")),
        name=name,
    )(x, w)


def _mm_resnorm_kernel(x_ref, w_ref, res_ref, wpost_ref, wnext_ref, h_ref, *t_ref, nk, col_chunk, row_chunk):
    kk = pl.program_id(1)
    tm, n = h_ref.shape

    @pl.when(kk == 0)
    def _():
        h_ref[...] = jnp.zeros_like(h_ref)

    x = x_ref[...]
    for c in range(n // col_chunk):
        sl = slice(c * col_chunk, (c + 1) * col_chunk)
        h_ref[:, sl] += _dot(x, w_ref[:, sl])

    @pl.when(kk == nk - 1)
    def _():
        def body(r, carry):
            rows = pl.ds(pl.multiple_of(r * row_chunk, row_chunk), row_chunk)
            mval = h_ref[rows, :]
            ms = jnp.mean(mval * mval, axis=-1, keepdims=True)
            h = res_ref[rows, :] + mval * lax.rsqrt(ms + NORM_EPS) * wpost_ref[...]
            h_ref[rows, :] = h
            if t_ref:
                hs = jnp.mean(h * h, axis=-1, keepdims=True)
                t_ref[0][rows, :] = (h * lax.rsqrt(hs + NORM_EPS) * wnext_ref[...]).astype(BF16)
            return carry

        lax.fori_loop(0, tm // row_chunk, body, 0, unroll=4)


def _mm_resnorm(x, w, layer, res, w_post, w_next, name):
    m, k = x.shape
    n = w.shape[2]
    tm = _tile(m, 512)
    deep = k >= 4 * n
    tk = _tile(k, 1024 if deep else 512)
    nk = k // tk
    emit_t = w_next is not None
    wn = (w_next if emit_t else w_post).reshape(1, n)
    out_shape = [jax.ShapeDtypeStruct((m, n), F32)]
    out_specs = [pl.BlockSpec((tm, n), lambda i, j: (i, 0))]
    if emit_t:
        out_shape.append(jax.ShapeDtypeStruct((m, n), BF16))
        out_specs.append(pl.BlockSpec((tm, n), lambda i, j: (i, 0)))
    outs = pl.pallas_call(
        functools.partial(_mm_resnorm_kernel, nk=nk, col_chunk=_tile(n, 1024), row_chunk=_tile(tm, 32, 8)),
        grid=(m // tm, nk),
        in_specs=[pl.BlockSpec((tm, tk), lambda i, j: (i, j)),
                  pl.BlockSpec((None, tk, n), lambda i, j: (layer, j, 0)),
                  pl.BlockSpec((tm, n), lambda i, j: (i, 0), pipeline_mode=pl.Buffered(1 if deep else 2)),
                  pl.BlockSpec((1, n), lambda i, j: (0, 0)),
                  pl.BlockSpec((1, n), lambda i, j: (0, 0))],
        out_specs=out_specs,
        out_shape=out_shape,
        compiler_params=_params(("parallel", "arbitrary")),
        name=name,
    )(x, w, res, w_post.reshape(1, n), wn)
    return (outs[0], outs[1]) if emit_t else (outs[0], None)


def _sgu_kernel(u_ref, v_ref, lnw_ref, lnb_ref, ws_ref, bs_ref, o_ref, vn_ref, *, groups, row_chunk):
    tm, e = v_ref.shape
    gd = e // groups

    def ln_body(r, carry):
        rows = pl.ds(pl.multiple_of(r * row_chunk, row_chunk), row_chunk)
        v = v_ref[rows, :].astype(F32)
        mu = jnp.mean(v, axis=-1, keepdims=True)
        xc = v - mu
        var = jnp.mean(xc * xc, axis=-1, keepdims=True)
        vn_ref[rows, :] = (xc * lax.rsqrt(var + LN_EPS) * lnw_ref[...] + lnb_ref[...]).astype(BF16)
        return carry

    lax.fori_loop(0, tm // row_chunk, ln_body, 0, unroll=2)

    ri = lax.broadcasted_iota(jnp.int32, (SG_CHUNK, SG_CHUNK), 0)
    ci = lax.broadcasted_iota(jnp.int32, (SG_CHUNK, SG_CHUNK), 1)
    causal = ri >= ci

    for g in range(groups):
        w = jnp.where(causal, ws_ref[g], jnp.zeros((), BF16))
        bias = bs_ref[g]
        cols = slice(g * gd, (g + 1) * gd)
        for c in range(tm // SG_CHUNK):
            rows = slice(c * SG_CHUNK, (c + 1) * SG_CHUNK)
            s = _dot(w, vn_ref[rows, cols]) + bias
            o_ref[rows, cols] = (u_ref[rows, cols].astype(F32) * s).astype(BF16)


def _sgu(z, ln_w, ln_b, w_s, b_s):
    m, e2 = z.shape
    e = e2 // 2
    groups, c, _ = w_s.shape
    assert c == SG_CHUNK
    tm = _tile(m, 256, SG_CHUNK)
    return pl.pallas_call(
        functools.partial(_sgu_kernel, groups=groups, row_chunk=_tile(tm, 32, 8)),
        grid=(m // tm,),
        in_specs=[pl.BlockSpec((tm, e), lambda i: (i, 0)),
                  pl.BlockSpec((tm, e), lambda i: (i, 1)),
                  pl.BlockSpec((1, e), lambda i: (0, 0)),
                  pl.BlockSpec((1, e), lambda i: (0, 0)),
                  pl.BlockSpec((groups, c, c), lambda i: (0, 0, 0)),
                  pl.BlockSpec((groups, c, 1), lambda i: (0, 0, 0))],
        out_specs=pl.BlockSpec((tm, e), lambda i: (i, 0)),
        out_shape=jax.ShapeDtypeStruct((m, e), BF16),
        scratch_shapes=[pltpu.VMEM((tm, e), BF16)],
        compiler_params=_params(("parallel",)),
        name="sgu",
    )(z, z, ln_w.reshape(1, e), ln_b.reshape(1, e), w_s.astype(BF16), b_s.reshape(groups, c, 1))


def _chunk_cumsum(x, axis):
    pos = lax.broadcasted_iota(jnp.int32, x.shape, axis) % DN_CHUNK
    sh = 1
    while sh < DN_CHUNK:
        x = x + jnp.where(pos >= sh, pltpu.roll(x, sh, axis), 0.0)
        sh *= 2
    return x


def _gates_kernel(t_ref, w_ref, ac_ref, dc_ref, gcol_ref, grow_ref, *, heads):
    ba = _dot(t_ref[...], w_ref[...])
    beta = _sigmoid(ba)
    g = -jnp.exp(ac_ref[...]) * _softplus(ba + dc_ref[...])
    gc = _chunk_cumsum(g, 0)
    lane = lax.broadcasted_iota(jnp.int32, ba.shape, 1)
    gcol_ref[...] = jnp.where(lane < heads, beta, gc)
    grow_ref[...] = gc.T[heads:, :]


def _gates(t, w, layer, col0, a_log, dt_bias):
    m, d = t.shape
    heads = a_log.shape[0]
    assert 2 * heads == LANES and col0 % LANES == 0
    tm = _tile(m, 512)
    zeros = jnp.zeros((heads,), F32)
    ac = jnp.concatenate([zeros, a_log]).reshape(1, 2 * heads)
    dc = jnp.concatenate([zeros, dt_bias]).reshape(1, 2 * heads)
    return pl.pallas_call(
        functools.partial(_gates_kernel, heads=heads),
        grid=(m // tm,),
        in_specs=[pl.BlockSpec((tm, d), lambda i: (i, 0)),
                  pl.BlockSpec((None, d, 2 * heads), lambda i: (layer, 0, col0 // LANES)),
                  pl.BlockSpec((1, 2 * heads), lambda i: (0, 0)),
                  pl.BlockSpec((1, 2 * heads), lambda i: (0, 0))],
        out_specs=[pl.BlockSpec((tm, 2 * heads), lambda i: (i, 0)),
                   pl.BlockSpec((heads, tm), lambda i: (0, i))],
        out_shape=[jax.ShapeDtypeStruct((m, 2 * heads), F32),
                   jax.ShapeDtypeStruct((heads, m), F32)],
        compiler_params=_params(("parallel",)),
        name="dn_gates",
    )(t, w, ac, dc)


def _delta_kernel(q_ref, k_ref, v_ref, z_ref, gcol_ref, grow_ref, cwq_ref, cwk_ref, cwv_ref, nw_ref,
                  o_ref, xpad_ref, s_ref, *, heads, rep, kh):
    j = pl.program_id(1)
    n = pl.program_id(2)
    tb = q_ref.shape[0]
    hd = nw_ref.shape[1]
    hs = kh * rep
    qw = kh * hd
    width = 2 * qw + hs * hd
    c_len = DN_CHUNK
    nc = tb // c_len

    @pl.when(n == 0)
    def _():
        xpad_ref[0:CONV_HALO, :] = jnp.zeros((CONV_HALO, width), F32)
        s_ref[...] = jnp.zeros_like(s_ref)

    @pl.when(n > 0)
    def _():
        xpad_ref[0:CONV_HALO, :] = xpad_ref[tb:tb + CONV_HALO, :]

    xpad_ref[CONV_HALO:, 0:qw] = q_ref[...].astype(F32)
    xpad_ref[CONV_HALO:, qw:2 * qw] = k_ref[...].astype(F32)
    xpad_ref[CONV_HALO:, 2 * qw:] = v_ref[...].astype(F32)

    cw = jnp.concatenate([cwq_ref[...], cwk_ref[...], cwv_ref[...]], axis=1)
    y = jnp.zeros((tb, width), F32)
    for i in range(DN_CONV):
        y = y + cw[i:i + 1, :] * xpad_ref[pl.ds(CONV_HALO - (DN_CONV - 1) + i, tb), :]
    y = y * _sigmoid(y)

    qn, kn = [], []
    for a in range(kh):
        qa = y[:, a * hd:(a + 1) * hd]
        ka = y[:, qw + a * hd:qw + (a + 1) * hd]
        qn.append(qa * lax.rsqrt(jnp.sum(qa * qa, axis=-1, keepdims=True) + L2_EPS) * (hd ** -0.5))
        kn.append(ka * lax.rsqrt(jnp.sum(ka * ka, axis=-1, keepdims=True) + L2_EPS))

    gsel = pltpu.roll(gcol_ref[...], (2 * heads - hs * j) % (2 * heads), 1)
    grow = grow_ref[...]

    ri = lax.broadcasted_iota(jnp.int32, (c_len, c_len), 0)
    ci = lax.broadcasted_iota(jnp.int32, (c_len, c_len), 1)
    tril = ri >= ci
    strict = ri > ci
    eye = (ri == ci).astype(F32)
    lvl_masks = []
    s = 1
    while s < c_len:
        lvl_masks.append(((ri // (2 * s)) == (ci // (2 * s))) & ((ri // s) != (ci // s)) & strict)
        s *= 2
    nw = nw_ref[...]

    pairs = [(a, i, c) for c in range(nc) for a in range(kh) for i in range(rep)]
    rows = [slice(c * c_len, (c + 1) * c_len) for c in range(nc)]

    qkk = {}
    for c in range(nc):
        for a in range(kh):
            kb = kn[a][rows[c]].astype(BF16)
            lhs = jnp.concatenate([qn[a][rows[c]].astype(BF16), kb], axis=0)
            qkk[a, c] = pl.dot(lhs, kb, trans_b=True)

    bcol, gcol, dec, amat, dmat = {}, {}, {}, {}, {}
    for p in pairs:
        a, i, c = p
        hl = a * rep + i
        bcol[p] = gsel[rows[c], hl:hl + 1]
        gcol[p] = gsel[rows[c], heads + hl:heads + hl + 1]
        gr = grow[hl:hl + 1, rows[c]]
        dec[p] = jnp.where(tril, jnp.exp(jnp.where(tril, gcol[p] - gr, 0.0)), 0.0)
        amat[p] = jnp.where(strict, bcol[p] * qkk[a, c][c_len:] * dec[p], 0.0)
        dmat[p] = eye - jnp.where(lvl_masks[0], amat[p], 0.0)

    ab = {p: amat[p].astype(BF16) for p in pairs}
    for mask in lvl_masks[1:]:
        db = {p: dmat[p].astype(BF16) for p in pairs}
        x = {p: _dot(db[p], jnp.where(mask, ab[p], jnp.zeros((), BF16))) for p in pairs}
        for p in pairs:
            dmat[p] = dmat[p] - _dot(x[p].astype(BF16), db[p])

    eg, rhs = {}, {}
    for p in pairs:
        a, i, c = p
        eg[p] = jnp.exp(gcol[p])
        vc = y[rows[c], 2 * qw + (a * rep + i) * hd:2 * qw + (a * rep + i + 1) * hd]
        rhs[p] = jnp.concatenate([bcol[p] * vc, (bcol[p] * eg[p]) * kn[a][rows[c]]], axis=1)
    sol = {p: (rhs[p] + _dot((dmat[p] - eye).astype(BF16), rhs[p].astype(BF16))).astype(BF16) for p in pairs}
    att = {p: _dot((qkk[p[0], p[2]][:c_len] * dec[p]).astype(BF16), sol[p]) for p in pairs}
    glast, ks = {}, {}
    for p in pairs:
        a, i, c = p
        glast[p] = gcol[p][c_len - 1:c_len, :]
        kd = (kn[a][rows[c]] * jnp.exp(glast[p] - gcol[p])).astype(BF16)
        ks[p] = pl.dot(kd, sol[p], trans_a=True)
    lhs2 = {}
    for p in pairs:
        a, i, c = p
        qeff = qn[a][rows[c]] * eg[p] - att[p][:, hd:]
        lhs2[p] = jnp.concatenate([qeff.astype(BF16), ks[p][:, hd:].astype(BF16)], axis=0)

    states = [s_ref[hl] for hl in range(hs)]
    for c in range(nc):
        r = {}
        for a in range(kh):
            for i in range(rep):
                hl = a * rep + i
                r[hl] = _dot(lhs2[a, i, c], states[hl].astype(BF16))
        for a in range(kh):
            for i in range(rep):
                hl = a * rep + i
                p = (a, i, c)
                o = att[p][:, :hd] + r[hl][:c_len]
                states[hl] = states[hl] * jnp.exp(glast[p]) - r[hl][c_len:] + ks[p][:, :hd]
                zc = z_ref[rows[c], hl * hd:(hl + 1) * hd].astype(F32)
                on = o * lax.rsqrt(jnp.mean(o * o, axis=-1, keepdims=True) + NORM_EPS) * nw * (zc * _sigmoid(zc))
                o_ref[rows[c], hl * hd:(hl + 1) * hd] = on.astype(BF16)
    for hl in range(hs):
        s_ref[hl] = states[hl]


def _delta(proj, gcol, grow, conv_w, norm_w, batch, seq, k_heads, v_heads):
    m = proj.shape[0]
    hd = norm_w.shape[0]
    rep = v_heads // k_heads
    hs = SUBLANES
    kh = hs // rep
    assert v_heads == rep * k_heads and hs == kh * rep and k_heads % kh == 0 and 2 * v_heads == LANES
    tb = _tile(seq, 256, DN_CHUNK)
    nt = seq // tb
    qblk, vblk = kh * hd, hs * hd
    k0 = k_heads // kh
    v0 = 2 * k_heads // hs
    z0 = v0 + v_heads // hs
    tok = lambda b, j, n: b * nt + n
    return pl.pallas_call(
        functools.partial(_delta_kernel, heads=v_heads, rep=rep, kh=kh),
        grid=(batch, k_heads // kh, nt),
        in_specs=[pl.BlockSpec((tb, qblk), lambda b, j, n: (tok(b, j, n), j)),
                  pl.BlockSpec((tb, qblk), lambda b, j, n: (tok(b, j, n), k0 + j)),
                  pl.BlockSpec((tb, vblk), lambda b, j, n: (tok(b, j, n), v0 + j)),
                  pl.BlockSpec((tb, vblk), lambda b, j, n: (tok(b, j, n), z0 + j)),
                  pl.BlockSpec((tb, 2 * v_heads), lambda b, j, n: (tok(b, j, n), 0)),
                  pl.BlockSpec((hs, tb), lambda b, j, n: (j, tok(b, j, n))),
                  pl.BlockSpec((DN_CONV, qblk), lambda b, j, n: (0, j)),
                  pl.BlockSpec((DN_CONV, qblk), lambda b, j, n: (0, k0 + j)),
                  pl.BlockSpec((DN_CONV, vblk), lambda b, j, n: (0, v0 + j)),
                  pl.BlockSpec((1, hd), lambda b, j, n: (0, 0))],
        out_specs=pl.BlockSpec((tb, vblk), lambda b, j, n: (tok(b, j, n), j)),
        out_shape=jax.ShapeDtypeStruct((m, v_heads * hd), BF16),
        scratch_shapes=[pltpu.VMEM((tb + CONV_HALO, 2 * qblk + vblk), F32),
                        pltpu.VMEM((hs, hd, hd), F32)],
        compiler_params=_params(("parallel", "parallel", "arbitrary")),
        name="delta_rule",
    )(proj, proj, proj, proj, gcol, grow, conv_w, conv_w, conv_w, norm_w.reshape(1, hd))


def kernel(x, norm_w, sg_w_in, sg_ln_w, sg_ln_b, sg_w_s, sg_b_s, sg_w_out, dn_w_in, dn_conv_w, dn_a_log,
           dn_dt_bias, dn_norm_w, dn_w_out, mlp_w_up, mlp_w_down):
    batch, seq, d = x.shape
    m = batch * seq
    depth = norm_w.shape[0]
    h = x.reshape(m, d)
    sg_w_in, sg_w_out, dn_w_in, dn_w_out, mlp_w_up, mlp_w_down = (
        w.astype(BF16) for w in (sg_w_in, sg_w_out, dn_w_in, dn_w_out, mlp_w_up, mlp_w_down))
    t = _rms_cast(h, norm_w[0, 0])
    for i in range(depth):
        j = i // 2
        if i % 2 == 0:
            z = _mm_act(t, sg_w_in, j, sg_w_in.shape[2], "gelu", "sg_in_gelu")
            mix = _sgu(z, sg_ln_w[j], sg_ln_b[j], sg_w_s[j], sg_b_s[j])
            w_out = sg_w_out
        else:
            v_heads = dn_a_log.shape[1]
            hd = dn_norm_w.shape[1]
            v_dim = dn_w_out.shape[1]
            conv_ch = dn_conv_w.shape[2]
            k_heads = (conv_ch - v_dim) // (2 * hd)
            n_main = conv_ch + v_dim
            proj = _mm_act(t, dn_w_in, j, n_main, None, "dn_in")
            gcol, grow = _gates(t, dn_w_in, j, n_main, dn_a_log[j], dn_dt_bias[j])
            mix = _delta(proj, gcol, grow, dn_conv_w[j], dn_norm_w[j], batch, seq, k_heads, v_heads)
            w_out = dn_w_out
        h, t = _mm_resnorm(mix, w_out, j, h, norm_w[i, 1], norm_w[i, 2], "mix_out")
        hid = _mm_act(t, mlp_w_up, i, mlp_w_up.shape[2], "relu2", "mlp_up")
        w_next = norm_w[i + 1, 0] if i + 1 < depth else None
        h, t = _mm_resnorm(hid, mlp_w_down, i, h, norm_w[i, 3], w_next, "mlp_down")
    return h.reshape(batch, seq, d)
```

```python
import functools
import math

import jax
import jax.numpy as jnp
from jax import lax
from jax.experimental import pallas as pl
from jax.experimental.pallas import tpu as pltpu

NORM_EPS = 1e-6
LN_EPS = 1e-5
L2_EPS = 1e-6
SG_CHUNK = 128
DN_CHUNK = 64
DN_CONV = 4
CONV_HALO = 8
SUBLANES = 8
LANES = 128
V7X_VMEM_LIMIT_BYTES = 60000 * 1024

F32 = jnp.float32
BF16 = jnp.bfloat16


def _tile(dim, pref, align=LANES):
    if dim <= pref:
        return dim
    t = (pref // align) * align
    while t > align and dim % t:
        t -= align
    assert dim % t == 0, (dim, pref)
    return t


def _params(sem):
    return pltpu.CompilerParams(dimension_semantics=sem, vmem_limit_bytes=V7X_VMEM_LIMIT_BYTES)


def _sigmoid(x):
    return 1.0 / (1.0 + jnp.exp(-x))


def _softplus(x):
    return jnp.maximum(x, 0.0) + jnp.log1p(jnp.exp(-jnp.abs(x)))


def _dot(a, b):
    return jnp.dot(a, b, preferred_element_type=F32)


def _rms_cast_kernel(x_ref, w_ref, o_ref):
    x = x_ref[...]
    ms = jnp.mean(x * x, axis=-1, keepdims=True)
    o_ref[...] = (x * lax.rsqrt(ms + NORM_EPS) * w_ref[...]).astype(o_ref.dtype)


def _rms_cast(x, w):
    m, d = x.shape
    tm = _tile(m, 256, 8)
    return pl.pallas_call(
        _rms_cast_kernel,
        grid=(m // tm,),
        in_specs=[pl.BlockSpec((tm, d), lambda i: (i, 0)),
                  pl.BlockSpec((1, d), lambda i: (0, 0))],
        out_specs=pl.BlockSpec((tm, d), lambda i: (i, 0)),
        out_shape=jax.ShapeDtypeStruct((m, d), BF16),
        compiler_params=_params(("parallel",)),
        name="rms_cast",
    )(x, w.reshape(1, d))


def _mm_act_kernel(x_ref, w_ref, o_ref, *, act):
    acc = _dot(x_ref[...], w_ref[...])
    if act == "gelu":
        acc = 0.5 * acc * (1.0 + lax.erf(acc * math.sqrt(0.5)))
    elif act == "relu2":
        r = jnp.maximum(acc, 0.0)
        acc = r * r
    o_ref[...] = acc.astype(o_ref.dtype)


def _mm_act(x, w, layer, n, act, name):
    m, k = x.shape
    tm = _tile(m, 1024)
    tn = _tile(n, 1024)
    return pl.pallas_call(
        functools.partial(_mm_act_kernel, act=act),
        grid=(m // tm, n // tn),
        in_specs=[pl.BlockSpec((tm, k), lambda i, j: (i, 0)),
                  pl.BlockSpec((None, k, tn), lambda i, j: (layer, 0, j))],
        out_specs=pl.BlockSpec((tm, tn), lambda i, j: (i, j)),
        out_shape=jax.ShapeDtypeStruct((m, n), BF16),
        compiler_params=_params(("parallel", "arbitrary")),
        name=name,
    )(x, w)


def _mm_resnorm_kernel(x_ref, *refs, nk, col_chunk, row_chunk, gate_hd):
    if gate_hd:
        zg_ref, nw_ref, w_ref, res_ref, wpost_ref, wnext_ref, h_ref, *t_ref = refs
    else:
        w_ref, res_ref, wpost_ref, wnext_ref, h_ref, *t_ref = refs
    kk = pl.program_id(1)
    tm, n = h_ref.shape

    @pl.when(kk == 0)
    def _():
        h_ref[...] = jnp.zeros_like(h_ref)

    x = x_ref[...]
    if gate_hd:
        parts = []
        for g in range(x.shape[1] // gate_hd):
            cols = slice(g * gate_hd, (g + 1) * gate_hd)
            o = x[:, cols].astype(F32)
            zc = zg_ref[:, cols].astype(F32)
            on = o * lax.rsqrt(jnp.mean(o * o, axis=-1, keepdims=True) + NORM_EPS) * nw_ref[...] * (zc * _sigmoid(zc))
            parts.append(on.astype(BF16))
        x = jnp.concatenate(parts, axis=1)
    for c in range(n // col_chunk):
        sl = slice(c * col_chunk, (c + 1) * col_chunk)
        h_ref[:, sl] += _dot(x, w_ref[:, sl])

    @pl.when(kk == nk - 1)
    def _():
        def body(r, carry):
            rows = pl.ds(pl.multiple_of(r * row_chunk, row_chunk), row_chunk)
            mval = h_ref[rows, :]
            ms = jnp.mean(mval * mval, axis=-1, keepdims=True)
            h = res_ref[rows, :] + mval * lax.rsqrt(ms + NORM_EPS) * wpost_ref[...]
            h_ref[rows, :] = h
            if t_ref:
                hs = jnp.mean(h * h, axis=-1, keepdims=True)
                t_ref[0][rows, :] = (h * lax.rsqrt(hs + NORM_EPS) * wnext_ref[...]).astype(BF16)
            return carry

        lax.fori_loop(0, tm // row_chunk, body, 0, unroll=4)


def _mm_resnorm(x, w, layer, res, w_post, w_next, name, gate=None):
    m, k = x.shape
    n = w.shape[2]
    tm = _tile(m, 512)
    deep = k >= 4 * n
    tk = _tile(k, 1024 if deep else 512)
    nk = k // tk
    emit_t = w_next is not None
    wn = (w_next if emit_t else w_post).reshape(1, n)
    out_shape = [jax.ShapeDtypeStruct((m, n), F32)]
    out_specs = [pl.BlockSpec((tm, n), lambda i, j: (i, 0))]
    if emit_t:
        out_shape.append(jax.ShapeDtypeStruct((m, n), BF16))
        out_specs.append(pl.BlockSpec((tm, n), lambda i, j: (i, 0)))
    gate_specs, gate_args, gate_hd = [], [], 0
    if gate is not None:
        z_src, z_col0, head_w = gate
        gate_hd = head_w.shape[0]
        assert z_col0 % tk == 0 and tk % gate_hd == 0
        gate_specs = [pl.BlockSpec((tm, tk), lambda i, j: (i, z_col0 // tk + j)),
                      pl.BlockSpec((1, gate_hd), lambda i, j: (0, 0))]
        gate_args = [z_src, head_w.reshape(1, gate_hd)]
    outs = pl.pallas_call(
        functools.partial(_mm_resnorm_kernel, nk=nk, col_chunk=_tile(n, 1024), row_chunk=_tile(tm, 32, 8),
                          gate_hd=gate_hd),
        grid=(m // tm, nk),
        in_specs=[pl.BlockSpec((tm, tk), lambda i, j: (i, j))] + gate_specs + [
                  pl.BlockSpec((None, tk, n), lambda i, j: (layer, j, 0)),
                  pl.BlockSpec((tm, n), lambda i, j: (i, 0), pipeline_mode=pl.Buffered(1 if deep else 2)),
                  pl.BlockSpec((1, n), lambda i, j: (0, 0)),
                  pl.BlockSpec((1, n), lambda i, j: (0, 0))],
        out_specs=out_specs,
        out_shape=out_shape,
        compiler_params=_params(("parallel", "arbitrary")),
        name=name,
    )(x, *gate_args, w, res, w_post.reshape(1, n), wn)
    return (outs[0], outs[1]) if emit_t else (outs[0], None)


def _sgu_kernel(u_ref, v_ref, lnw_ref, lnb_ref, ws_ref, bs_ref, o_ref, vn_ref, *, groups, row_chunk):
    tm, e = v_ref.shape
    gd = e // groups

    def ln_body(r, carry):
        rows = pl.ds(pl.multiple_of(r * row_chunk, row_chunk), row_chunk)
        v = v_ref[rows, :].astype(F32)
        mu = jnp.mean(v, axis=-1, keepdims=True)
        xc = v - mu
        var = jnp.mean(xc * xc, axis=-1, keepdims=True)
        vn_ref[rows, :] = (xc * lax.rsqrt(var + LN_EPS) * lnw_ref[...] + lnb_ref[...]).astype(BF16)
        return carry

    lax.fori_loop(0, tm // row_chunk, ln_body, 0, unroll=2)

    ri = lax.broadcasted_iota(jnp.int32, (SG_CHUNK, SG_CHUNK), 0)
    ci = lax.broadcasted_iota(jnp.int32, (SG_CHUNK, SG_CHUNK), 1)
    causal = ri >= ci

    for g in range(groups):
        w = jnp.where(causal, ws_ref[g], jnp.zeros((), BF16))
        bias = bs_ref[g]
        cols = slice(g * gd, (g + 1) * gd)
        for c in range(tm // SG_CHUNK):
            rows = slice(c * SG_CHUNK, (c + 1) * SG_CHUNK)
            s = _dot(w, vn_ref[rows, cols]) + bias
            o_ref[rows, cols] = (u_ref[rows, cols].astype(F32) * s).astype(BF16)


def _sgu(z, ln_w, ln_b, w_s, b_s):
    m, e2 = z.shape
    e = e2 // 2
    groups, c, _ = w_s.shape
    assert c == SG_CHUNK
    tm = _tile(m, 256, SG_CHUNK)
    return pl.pallas_call(
        functools.partial(_sgu_kernel, groups=groups, row_chunk=_tile(tm, 32, 8)),
        grid=(m // tm,),
        in_specs=[pl.BlockSpec((tm, e), lambda i: (i, 0)),
                  pl.BlockSpec((tm, e), lambda i: (i, 1)),
                  pl.BlockSpec((1, e), lambda i: (0, 0)),
                  pl.BlockSpec((1, e), lambda i: (0, 0)),
                  pl.BlockSpec((groups, c, c), lambda i: (0, 0, 0)),
                  pl.BlockSpec((groups, c, 1), lambda i: (0, 0, 0))],
        out_specs=pl.BlockSpec((tm, e), lambda i: (i, 0)),
        out_shape=jax.ShapeDtypeStruct((m, e), BF16),
        scratch_shapes=[pltpu.VMEM((tm, e), BF16)],
        compiler_params=_params(("parallel",)),
        name="sgu",
    )(z, z, ln_w.reshape(1, e), ln_b.reshape(1, e), w_s.astype(BF16), b_s.reshape(groups, c, 1))


def _chunk_cumsum(x, axis):
    pos = lax.broadcasted_iota(jnp.int32, x.shape, axis) % DN_CHUNK
    sh = 1
    while sh < DN_CHUNK:
        x = x + jnp.where(pos >= sh, pltpu.roll(x, sh, axis), 0.0)
        sh *= 2
    return x


def _gates_kernel(t_ref, w_ref, ac_ref, dc_ref, gcol_ref, grow_ref, *, heads):
    ba = _dot(t_ref[...], w_ref[...])
    beta = _sigmoid(ba)
    g = -jnp.exp(ac_ref[...]) * _softplus(ba + dc_ref[...])
    gc = _chunk_cumsum(g, 0)
    lane = lax.broadcasted_iota(jnp.int32, ba.shape, 1)
    gcol_ref[...] = jnp.where(lane < heads, beta, gc)
    grow_ref[...] = gc.T[heads:, :]


def _gates(t, w, layer, col0, a_log, dt_bias):
    m, d = t.shape
    heads = a_log.shape[0]
    assert 2 * heads == LANES and col0 % LANES == 0
    tm = _tile(m, 512)
    zeros = jnp.zeros((heads,), F32)
    ac = jnp.concatenate([zeros, a_log]).reshape(1, 2 * heads)
    dc = jnp.concatenate([zeros, dt_bias]).reshape(1, 2 * heads)
    return pl.pallas_call(
        functools.partial(_gates_kernel, heads=heads),
        grid=(m // tm,),
        in_specs=[pl.BlockSpec((tm, d), lambda i: (i, 0)),
                  pl.BlockSpec((None, d, 2 * heads), lambda i: (layer, 0, col0 // LANES)),
                  pl.BlockSpec((1, 2 * heads), lambda i: (0, 0)),
                  pl.BlockSpec((1, 2 * heads), lambda i: (0, 0))],
        out_specs=[pl.BlockSpec((tm, 2 * heads), lambda i: (i, 0)),
                   pl.BlockSpec((heads, tm), lambda i: (0, i))],
        out_shape=[jax.ShapeDtypeStruct((m, 2 * heads), F32),
                   jax.ShapeDtypeStruct((heads, m), F32)],
        compiler_params=_params(("parallel",)),
        name="dn_gates",
    )(t, w, ac, dc)


def _delta_kernel(q_ref, k_ref, v_ref, gcol_ref, grow_ref, cwq_ref, cwk_ref, cwv_ref,
                  o_ref, xpad_ref, s_ref, *, heads, rep, kh, hd):
    j = pl.program_id(1)
    n = pl.program_id(2)
    tb = q_ref.shape[0]
    hs = kh * rep
    qw = kh * hd
    width = 2 * qw + hs * hd
    c_len = DN_CHUNK
    nc = tb // c_len

    @pl.when(n == 0)
    def _():
        xpad_ref[0:CONV_HALO, :] = jnp.zeros((CONV_HALO, width), F32)
        s_ref[...] = jnp.zeros_like(s_ref)

    @pl.when(n > 0)
    def _():
        xpad_ref[0:CONV_HALO, :] = xpad_ref[tb:tb + CONV_HALO, :]

    xpad_ref[CONV_HALO:, 0:qw] = q_ref[...].astype(F32)
    xpad_ref[CONV_HALO:, qw:2 * qw] = k_ref[...].astype(F32)
    xpad_ref[CONV_HALO:, 2 * qw:] = v_ref[...].astype(F32)

    cw = jnp.concatenate([cwq_ref[...], cwk_ref[...], cwv_ref[...]], axis=1)
    y = jnp.zeros((tb, width), F32)
    for i in range(DN_CONV):
        y = y + cw[i:i + 1, :] * xpad_ref[pl.ds(CONV_HALO - (DN_CONV - 1) + i, tb), :]
    y = y * _sigmoid(y)

    qn, kn = [], []
    for a in range(kh):
        qa = y[:, a * hd:(a + 1) * hd]
        ka = y[:, qw + a * hd:qw + (a + 1) * hd]
        qn.append(qa * lax.rsqrt(jnp.sum(qa * qa, axis=-1, keepdims=True) + L2_EPS) * (hd ** -0.5))
        kn.append(ka * lax.rsqrt(jnp.sum(ka * ka, axis=-1, keepdims=True) + L2_EPS))

    gsel = pltpu.roll(gcol_ref[...], (2 * heads - hs * j) % (2 * heads), 1)
    grow = grow_ref[...]

    ri = lax.broadcasted_iota(jnp.int32, (c_len, c_len), 0)
    ci = lax.broadcasted_iota(jnp.int32, (c_len, c_len), 1)
    tril = ri >= ci
    strict = ri > ci
    eye = (ri == ci).astype(F32)
    lvl_masks = []
    s = 1
    while s < c_len:
        lvl_masks.append(((ri // (2 * s)) == (ci // (2 * s))) & ((ri // s) != (ci // s)) & strict)
        s *= 2

    pairs =[(a, i, c) for c in range(nc) for a in range(kh) for i in range(rep)]
    rows = [slice(c * c_len, (c + 1) * c_len) for c in range(nc)]

    qkk = {}
    for c in range(nc):
        for a in range(kh):
            kb = kn[a][rows[c]].astype(BF16)
            lhs = jnp.concatenate([qn[a][rows[c]].astype(BF16), kb], axis=0)
            qkk[a, c] = pl.dot(lhs, kb, trans_b=True)

    bcol, gcol, dec, amat, dmat = {}, {}, {}, {}, {}
    for p in pairs:
        a, i, c = p
        hl = a * rep + i
        bcol[p] = gsel[rows[c], hl:hl + 1]
        gcol[p] = gsel[rows[c], heads + hl:heads + hl + 1]
        gr = grow[hl:hl + 1, rows[c]]
        dec[p] = jnp.where(tril, jnp.exp(jnp.where(tril, gcol[p] - gr, 0.0)), 0.0)
        amat[p] = jnp.where(strict, bcol[p] * qkk[a, c][c_len:] * dec[p], 0.0)
        dmat[p] = eye - jnp.where(lvl_masks[0], amat[p], 0.0)

    for mask in lvl_masks[1:]:
        db = {p: dmat[p].astype(BF16) for p in pairs}
        x = {p: _dot(db[p], jnp.where(mask, amat[p], 0.0).astype(BF16)) for p in pairs}
        for p in pairs:
            dmat[p] = dmat[p] - _dot(x[p].astype(BF16), db[p])

    eg, rhs = {}, {}
    for p in pairs:
        a, i, c = p
        eg[p] = jnp.exp(gcol[p])
        vc = y[rows[c], 2 * qw + (a * rep + i) * hd:2 * qw + (a * rep + i + 1) * hd]
        rhs[p] = jnp.concatenate([bcol[p] * vc, (bcol[p] * eg[p]) * kn[a][rows[c]]], axis=1)
    sol = {p: (rhs[p] + _dot((dmat[p] - eye).astype(BF16), rhs[p].astype(BF16))).astype(BF16) for p in pairs}
    att = {p: _dot((qkk[p[0], p[2]][:c_len] * dec[p]).astype(BF16), sol[p]) for p in pairs}
    glast, ks = {}, {}
    for p in pairs:
        a, i, c = p
        glast[p] = gcol[p][c_len - 1:c_len, :]
        kd = (kn[a][rows[c]] * jnp.exp(glast[p] - gcol[p])).astype(BF16)
        ks[p] = pl.dot(kd, sol[p], trans_a=True)
    lhs2 = {}
    for p in pairs:
        a, i, c = p
        qeff = qn[a][rows[c]] * eg[p] - att[p][:, hd:]
        lhs2[p] = jnp.concatenate([qeff.astype(BF16), ks[p][:, hd:].astype(BF16)], axis=0)

    states = [s_ref[hl] for hl in range(hs)]
    for c in range(nc):
        r = {}
        for a in range(kh):
            for i in range(rep):
                hl = a * rep + i
                r[hl] = _dot(lhs2[a, i, c], states[hl].astype(BF16))
        for a in range(kh):
            for i in range(rep):
                hl = a * rep + i
                p = (a, i, c)
                o = att[p][:, :hd] + r[hl][:c_len]
                states[hl] = states[hl] * jnp.exp(glast[p]) - r[hl][c_len:] + ks[p][:, :hd]
                o_ref[rows[c], hl * hd:(hl + 1) * hd] = o.astype(BF16)
    for hl in range(hs):
        s_ref[hl] = states[hl]


def _delta(proj, gcol, grow, conv_w, hd, batch, seq, k_heads, v_heads):
    m = proj.shape[0]
    rep = v_heads // k_heads
    hs = SUBLANES
    kh = hs // rep
    assert v_heads == rep * k_heads and hs == kh * rep and k_heads % kh == 0 and 2 * v_heads == LANES
    tb = _tile(seq, 256, DN_CHUNK)
    nt = seq // tb
    qblk, vblk = kh * hd, hs * hd
    k0 = k_heads // kh
    v0 = 2 * k_heads // hs
    tok = lambda b, j, n: b * nt + n
    return pl.pallas_call(
        functools.partial(_delta_kernel, heads=v_heads, rep=rep, kh=kh, hd=hd),
        grid=(batch, k_heads // kh, nt),
        in_specs=[pl.BlockSpec((tb, qblk), lambda b, j, n: (tok(b, j, n), j)),
                  pl.BlockSpec((tb, qblk), lambda b, j, n: (tok(b, j, n), k0 + j)),
                  pl.BlockSpec((tb, vblk), lambda b, j, n: (tok(b, j, n), v0 + j)),
                  pl.BlockSpec((tb, 2 * v_heads), lambda b, j, n: (tok(b, j, n), 0)),
                  pl.BlockSpec((hs, tb), lambda b, j, n: (j, tok(b, j, n))),
                  pl.BlockSpec((DN_CONV, qblk), lambda b, j, n: (0, j)),
                  pl.BlockSpec((DN_CONV, qblk), lambda b, j, n: (0, k0 + j)),
                  pl.BlockSpec((DN_CONV, vblk), lambda b, j, n: (0, v0 + j))],
        out_specs=pl.BlockSpec((tb, vblk), lambda b, j, n: (tok(b, j, n), j)),
        out_shape=jax.ShapeDtypeStruct((m, v_heads * hd), BF16),
        scratch_shapes=[pltpu.VMEM((tb + CONV_HALO, 2 * qblk + vblk), F32),
                        pltpu.VMEM((hs, hd, hd), F32)],
        compiler_params=_params(("parallel", "parallel", "arbitrary")),
        name="delta_rule",
    )(proj, proj, proj, gcol, grow, conv_w, conv_w, conv_w)


def kernel(x, norm_w, sg_w_in, sg_ln_w, sg_ln_b, sg_w_s, sg_b_s, sg_w_out, dn_w_in, dn_conv_w, dn_a_log,
           dn_dt_bias, dn_norm_w, dn_w_out, mlp_w_up, mlp_w_down):
    batch, seq, d = x.shape
    m = batch * seq
    depth = norm_w.shape[0]
    h = x.reshape(m, d)
    sg_w_in, sg_w_out, dn_w_in, dn_w_out, mlp_w_up, mlp_w_down = (
        w.astype(BF16) for w in (sg_w_in, sg_w_out, dn_w_in, dn_w_out, mlp_w_up, mlp_w_down))
    t = _rms_cast(h, norm_w[0, 0])
    for i in range(depth):
        j = i // 2
        if i % 2 == 0:
            z = _mm_act(t, sg_w_in, j, sg_w_in.shape[2], "gelu", "sg_in_gelu")
            mix = _sgu(z, sg_ln_w[j], sg_ln_b[j], sg_w_s[j], sg_b_s[j])
            w_out, gate = sg_w_out, None
        else:
            v_heads = dn_a_log.shape[1]
            hd = dn_norm_w.shape[1]
            v_dim = dn_w_out.shape[1]
            conv_ch = dn_conv_w.shape[2]
            k_heads = (conv_ch - v_dim) // (2 * hd)
            n_main = conv_ch + v_dim
            proj = _mm_act(t, dn_w_in, j, n_main, None, "dn_in")
            gcol, grow = _gates(t, dn_w_in, j, n_main, dn_a_log[j], dn_dt_bias[j])
            mix = _delta(proj, gcol, grow, dn_conv_w[j], hd, batch, seq, k_heads, v_heads)
            w_out, gate = dn_w_out, (proj, conv_ch, dn_norm_w[j])
        h, t = _mm_resnorm(mix, w_out, j, h, norm_w[i, 1], norm_w[i, 2], "mix_out", gate)
        hid = _mm_act(t, mlp_w_up, i, mlp_w_up.shape[2], "relu2", "mlp_up")
        w_next = norm_w[i + 1, 0] if i + 1 < depth else None
        h, t = _mm_resnorm(hid, mlp_w_down, i, h, norm_w[i, 3], w_next, "mlp_down")
    return h.reshape(batch, seq, d)
```

```python
import functools
import math

import jax
import jax.numpy as jnp
from jax import lax
from jax.experimental import pallas as pl
from jax.experimental.pallas import tpu as pltpu

NORM_EPS = 1e-6
LN_EPS = 1e-5
L2_EPS = 1e-6
SG_CHUNK = 128
DN_CHUNK = 64
DN_CONV = 4
CONV_HALO = 8
SUBLANES = 8
LANES = 128
V7X_VMEM_LIMIT_BYTES = 60000 * 1024

F32 = jnp.float32
BF16 = jnp.bfloat16


def _tile(dim, pref, align=LANES):
    if dim <= pref:
        return dim
    t = (pref // align) * align
    while t > align and dim % t:
        t -= align
    assert dim % t == 0, (dim, pref)
    return t


def _params(sem):
    return pltpu.CompilerParams(dimension_semantics=sem, vmem_limit_bytes=V7X_VMEM_LIMIT_BYTES)


def _sigmoid(x):
    return 1.0 / (1.0 + jnp.exp(-x))


def _softplus(x):
    return jnp.maximum(x, 0.0) + jnp.log1p(jnp.exp(-jnp.abs(x)))


def _dot(a, b):
    return jnp.dot(a, b, preferred_element_type=F32)


def _rms_cast_kernel(x_ref, w_ref, o_ref):
    x = x_ref[...]
    ms = jnp.mean(x * x, axis=-1, keepdims=True)
    o_ref[...] = (x * lax.rsqrt(ms + NORM_EPS) * w_ref[...]).astype(o_ref.dtype)


def _rms_cast(x, w):
    m, d = x.shape
    tm = _tile(m, 256, 8)
    return pl.pallas_call(
        _rms_cast_kernel,
        grid=(m // tm,),
        in_specs=[pl.BlockSpec((tm, d), lambda i: (i, 0)),
                  pl.BlockSpec((1, d), lambda i: (0, 0))],
        out_specs=pl.BlockSpec((tm, d), lambda i: (i, 0)),
        out_shape=jax.ShapeDtypeStruct((m, d), BF16),
        compiler_params=_params(("parallel",)),
        name="rms_cast",
    )(x, w.reshape(1, d))


def _mm_act_kernel(x_ref, w_ref, o_ref, *, act):
    acc = _dot(x_ref[...], w_ref[...])
    if act == "gelu":
        acc = 0.5 * acc * (1.0 + lax.erf(acc * math.sqrt(0.5)))
    elif act == "relu2":
        r = jnp.maximum(acc, 0.0)
        acc = r * r
    o_ref[...] = acc.astype(o_ref.dtype)


def _mm_act(x, w, layer, n, act, name):
    m, k = x.shape
    tm = _tile(m, 1024)
    tn = _tile(n, 1024)
    return pl.pallas_call(
        functools.partial(_mm_act_kernel, act=act),
        grid=(m // tm, n // tn),
        in_specs=[pl.BlockSpec((tm, k), lambda i, j: (i, 0)),
                  pl.BlockSpec((None, k, tn), lambda i, j: (layer, 0, j))],
        out_specs=pl.BlockSpec((tm, tn), lambda i, j: (i, j)),
        out_shape=jax.ShapeDtypeStruct((m, n), BF16),
        compiler_params=_params(("parallel", "arbitrary")),
        name=name,
    )(x, w)


def _mm_resnorm_kernel(*refs, nk, col_chunk, row_chunk, gate_hd):
    if gate_hd:
        (xn_ref, zn_ref, xf_ref, zf_ref, nw_ref, w_ref, res_ref, wpost_ref, wnext_ref,
         h_ref, t0_ref, xg0_ref, xg1_ref) = refs
        t_ref = (t0_ref,)
    else:
        x_ref, w_ref, res_ref, wpost_ref, wnext_ref, h_ref, *t_ref = refs
    kk = pl.program_id(1)
    tm, n = h_ref.shape

    @pl.when(kk == 0)
    def _():
        h_ref[...] = jnp.zeros_like(h_ref)

    def accumulate(x):
        for c in range(n // col_chunk):
            sl = slice(c * col_chunk, (c + 1) * col_chunk)
            h_ref[:, sl] += _dot(x, w_ref[:, sl])

    if gate_hd:
        def gated(o_ref, z_ref):
            parts = []
            for g in range(o_ref.shape[1] // gate_hd):
                cols = slice(g * gate_hd, (g + 1) * gate_hd)
                o = o_ref[:, cols].astype(F32)
                zc = z_ref[:, cols].astype(F32)
                on = (o * lax.rsqrt(jnp.mean(o * o, axis=-1, keepdims=True) + NORM_EPS) * nw_ref[...]
                      * (zc * _sigmoid(zc)))
                parts.append(on.astype(BF16))
            return jnp.concatenate(parts, axis=1)

        @pl.when((pl.program_id(0) == 0) & (kk == 0))
        def _():
            xg0_ref[...] = gated(xf_ref, zf_ref)

        @pl.when(kk % 2 == 0)
        def _():
            xg1_ref[...] = gated(xn_ref, zn_ref)
            accumulate(xg0_ref[...])

        @pl.when(kk % 2 == 1)
        def _():
            xg0_ref[...] = gated(xn_ref, zn_ref)
            accumulate(xg1_ref[...])
    else:
        accumulate(x_ref[...])

    @pl.when(kk == nk - 1)
    def _():
        def body(r, carry):
            rows = pl.ds(pl.multiple_of(r * row_chunk, row_chunk), row_chunk)
            mval = h_ref[rows, :]
            ms = jnp.mean(mval * mval, axis=-1, keepdims=True)
            h = res_ref[rows, :] + mval * lax.rsqrt(ms + NORM_EPS) * wpost_ref[...]
            h_ref[rows, :] = h
            if t_ref:
                hs = jnp.mean(h * h, axis=-1, keepdims=True)
                t_ref[0][rows, :] = (h * lax.rsqrt(hs + NORM_EPS) * wnext_ref[...]).astype(BF16)
            return carry

        lax.fori_loop(0, tm // row_chunk, body, 0, unroll=4)


def _mm_resnorm(x, w, layer, res, w_post, w_next, name, gate=None):
    m, k = x.shape
    n = w.shape[2]
    tm = _tile(m, 512)
    deep = k >= 4 * n
    tk = _tile(k, 1024 if deep else 512)
    nk = k // tk
    emit_t = w_next is not None
    wn = (w_next if emit_t else w_post).reshape(1, n)
    out_shape = [jax.ShapeDtypeStruct((m, n), F32)]
    out_specs = [pl.BlockSpec((tm, n), lambda i, j: (i, 0))]
    if emit_t:
        out_shape.append(jax.ShapeDtypeStruct((m, n), BF16))
        out_specs.append(pl.BlockSpec((tm, n), lambda i, j: (i, 0)))
    lhs_specs, lhs_args, gate_hd, scratch = [pl.BlockSpec((tm, tk), lambda i, j: (i, j))], [x], 0, []
    if gate is not None:
        z_src, z_col0, head_w = gate
        gate_hd = head_w.shape[0]
        assert z_col0 % tk == 0 and tk % gate_hd == 0 and nk % 2 == 0 and emit_t
        zb = z_col0 // tk
        last = (m // tm) * nk - 1

        def ahead(i, j):
            s = jnp.minimum(i * nk + j + 1, last)
            return s // nk, s % nk

        lhs_specs = [pl.BlockSpec((tm, tk), lambda i, j: ahead(i, j)),
                     pl.BlockSpec((tm, tk), lambda i, j: (ahead(i, j)[0], zb + ahead(i, j)[1])),
                     pl.BlockSpec((tm, tk), lambda i, j: (0, 0)),
                     pl.BlockSpec((tm, tk), lambda i, j: (0, zb)),
                     pl.BlockSpec((1, gate_hd), lambda i, j: (0, 0))]
        lhs_args = [x, z_src, x, z_src, head_w.reshape(1, gate_hd)]
        scratch = [pltpu.VMEM((tm, tk), BF16), pltpu.VMEM((tm, tk), BF16)]
    outs = pl.pallas_call(
        functools.partial(_mm_resnorm_kernel, nk=nk, col_chunk=_tile(n, 1024), row_chunk=_tile(tm, 32, 8),
                          gate_hd=gate_hd),
        grid=(m // tm, nk),
        in_specs=lhs_specs + [
                  pl.BlockSpec((None, tk, n), lambda i, j: (layer, j, 0)),
                  pl.BlockSpec((tm, n), lambda i, j: (i, 0), pipeline_mode=pl.Buffered(1 if deep else 2)),
                  pl.BlockSpec((1, n), lambda i, j: (0, 0)),
                  pl.BlockSpec((1, n), lambda i, j: (0, 0))],
        out_specs=out_specs,
        out_shape=out_shape,
        scratch_shapes=scratch,
        compiler_params=_params(("arbitrary", "arbitrary")),
        name=name,
    )(*lhs_args, w, res, w_post.reshape(1, n), wn)
    return (outs[0], outs[1]) if emit_t else (outs[0], None)


def _sgu_kernel(u_ref, v_ref, lnw_ref, lnb_ref, ws_ref, bs_ref, o_ref, vn_ref, *, groups, row_chunk):
    tm, e = v_ref.shape
    gd = e // groups

    def ln_body(r, carry):
        rows = pl.ds(pl.multiple_of(r * row_chunk, row_chunk), row_chunk)
        v = v_ref[rows, :].astype(F32)
        mu = jnp.mean(v, axis=-1, keepdims=True)
        xc = v - mu
        var = jnp.mean(xc * xc, axis=-1, keepdims=True)
        vn_ref[rows, :] = (xc * lax.rsqrt(var + LN_EPS) * lnw_ref[...] + lnb_ref[...]).astype(BF16)
        return carry

    lax.fori_loop(0, tm // row_chunk, ln_body, 0, unroll=2)

    ri = lax.broadcasted_iota(jnp.int32, (SG_CHUNK, SG_CHUNK), 0)
    ci = lax.broadcasted_iota(jnp.int32, (SG_CHUNK, SG_CHUNK), 1)
    causal = ri >= ci

    for g in range(groups):
        w = jnp.where(causal, ws_ref[g], jnp.zeros((), BF16))
        bias = bs_ref[g]
        cols = slice(g * gd, (g + 1) * gd)
        for c in range(tm // SG_CHUNK):
            rows = slice(c * SG_CHUNK, (c + 1) * SG_CHUNK)
            s = _dot(w, vn_ref[rows, cols]) + bias
            o_ref[rows, cols] = (u_ref[rows, cols].astype(F32) * s).astype(BF16)


def _sgu(z, ln_w, ln_b, w_s, b_s):
    m, e2 = z.shape
    e = e2 // 2
    groups, c, _ = w_s.shape
    assert c == SG_CHUNK
    tm = _tile(m, 256, SG_CHUNK)
    return pl.pallas_call(
        functools.partial(_sgu_kernel, groups=groups, row_chunk=_tile(tm, 32, 8)),
        grid=(m // tm,),
        in_specs=[pl.BlockSpec((tm, e), lambda i: (i, 0)),
                  pl.BlockSpec((tm, e), lambda i: (i, 1)),
                  pl.BlockSpec((1, e), lambda i: (0, 0)),
                  pl.BlockSpec((1, e), lambda i: (0, 0)),
                  pl.BlockSpec((groups, c, c), lambda i: (0, 0, 0)),
                  pl.BlockSpec((groups, c, 1), lambda i: (0, 0, 0))],
        out_specs=pl.BlockSpec((tm, e), lambda i: (i, 0)),
        out_shape=jax.ShapeDtypeStruct((m, e), BF16),
        scratch_shapes=[pltpu.VMEM((tm, e), BF16)],
        compiler_params=_params(("parallel",)),
        name="sgu",
    )(z, z, ln_w.reshape(1, e), ln_b.reshape(1, e), w_s.astype(BF16), b_s.reshape(groups, c, 1))


def _chunk_cumsum(x, axis):
    pos = lax.broadcasted_iota(jnp.int32, x.shape, axis) % DN_CHUNK
    sh = 1
    while sh < DN_CHUNK:
        x = x + jnp.where(pos >= sh, pltpu.roll(x, sh, axis), 0.0)
        sh *= 2
    return x


def _gates_kernel(t_ref, w_ref, ac_ref, dc_ref, gcol_ref, grow_ref, *, heads):
    ba = _dot(t_ref[...], w_ref[...])
    beta = _sigmoid(ba)
    g = -jnp.exp(ac_ref[...]) * _softplus(ba + dc_ref[...])
    gc = _chunk_cumsum(g, 0)
    lane = lax.broadcasted_iota(jnp.int32, ba.shape, 1)
    gcol_ref[...] = jnp.where(lane < heads, beta, gc)
    grow_ref[...] = gc.T[heads:, :]


def _gates(t, w, layer, col0, a_log, dt_bias):
    m, d = t.shape
    heads = a_log.shape[0]
    assert 2 * heads == LANES and col0 % LANES == 0
    tm = _tile(m, 512)
    zeros = jnp.zeros((heads,), F32)
    ac = jnp.concatenate([zeros, a_log]).reshape(1, 2 * heads)
    dc = jnp.concatenate([zeros, dt_bias]).reshape(1, 2 * heads)
    return pl.pallas_call(
        functools.partial(_gates_kernel, heads=heads),
        grid=(m // tm,),
        in_specs=[pl.BlockSpec((tm, d), lambda i: (i, 0)),
                  pl.BlockSpec((None, d, 2 * heads), lambda i: (layer, 0, col0 // LANES)),
                  pl.BlockSpec((1, 2 * heads), lambda i: (0, 0)),
                  pl.BlockSpec((1, 2 * heads), lambda i: (0, 0))],
        out_specs=[pl.BlockSpec((tm, 2 * heads), lambda i: (i, 0)),
                   pl.BlockSpec((heads, tm), lambda i: (0, i))],
        out_shape=[jax.ShapeDtypeStruct((m, 2 * heads), F32),
                   jax.ShapeDtypeStruct((heads, m), F32)],
        compiler_params=_params(("parallel",)),
        name="dn_gates",
    )(t, w, ac, dc)


def _delta_kernel(q_ref, k_ref, v_ref, gcol_ref, grow_ref, cwq_ref, cwk_ref, cwv_ref,
                  o_ref, xpad_ref, s_ref, *, heads, rep, kh, hd):
    j = pl.program_id(1)
    n = pl.program_id(2)
    tb = q_ref.shape[0]
    hs = kh * rep
    qw = kh * hd
    width = 2 * qw + hs * hd
    c_len = DN_CHUNK
    nc = tb // c_len

    @pl.when(n == 0)
    def _():
        xpad_ref[0:CONV_HALO, :] = jnp.zeros((CONV_HALO, width), F32)
        s_ref[...] = jnp.zeros_like(s_ref)

    @pl.when(n > 0)
    def _():
        xpad_ref[0:CONV_HALO, :] = xpad_ref[tb:tb + CONV_HALO, :]

    xpad_ref[CONV_HALO:, 0:qw] = q_ref[...].astype(F32)
    xpad_ref[CONV_HALO:, qw:2 * qw] = k_ref[...].astype(F32)
    xpad_ref[CONV_HALO:, 2 * qw:] = v_ref[...].astype(F32)

    cw = jnp.concatenate([cwq_ref[...], cwk_ref[...], cwv_ref[...]], axis=1)
    y = jnp.zeros((tb, width), F32)
    for i in range(DN_CONV):
        y = y + cw[i:i + 1, :] * xpad_ref[pl.ds(CONV_HALO - (DN_CONV - 1) + i, tb), :]
    y = y * _sigmoid(y)

    qn, kn = [], []
    for a in range(kh):
        qa = y[:, a * hd:(a + 1) * hd]
        ka = y[:, qw + a * hd:qw + (a + 1) * hd]
        qn.append(qa * lax.rsqrt(jnp.sum(qa * qa, axis=-1, keepdims=True) + L2_EPS) * (hd ** -0.5))
        kn.append(ka * lax.rsqrt(jnp.sum(ka * ka, axis=-1, keepdims=True) + L2_EPS))

    gsel = pltpu.roll(gcol_ref[...], (2 * heads - hs * j) % (2 * heads), 1)
    grow = grow_ref[...]

    ri = lax.broadcasted_iota(jnp.int32, (c_len, c_len), 0)
    ci = lax.broadcasted_iota(jnp.int32, (c_len, c_len), 1)
    tril = ri >= ci
    strict = ri > ci
    eye = (ri == ci).astype(F32)
    lvl_masks = []
    s = 1
    while s < c_len:
        lvl_masks.append(((ri // (2 * s)) == (ci // (2 * s))) & ((ri // s) != (ci // s)) & strict)
        s *= 2

    pairs =[(a, i, c) for c in range(nc) for a in range(kh) for i in range(rep)]
    rows = [slice(c * c_len, (c + 1) * c_len) for c in range(nc)]

    qkk = {}
    for c in range(nc):
        for a in range(kh):
            kb = kn[a][rows[c]].astype(BF16)
            lhs = jnp.concatenate([qn[a][rows[c]].astype(BF16), kb], axis=0)
            qkk[a, c] = pl.dot(lhs, kb, trans_b=True)

    bcol, gcol, dec, amat, dmat = {}, {}, {}, {}, {}
    for p in pairs:
        a, i, c = p
        hl = a * rep + i
        bcol[p] = gsel[rows[c], hl:hl + 1]
        gcol[p] = gsel[rows[c], heads + hl:heads + hl + 1]
        gr = grow[hl:hl + 1, rows[c]]
        dec[p] = jnp.where(tril, jnp.exp(jnp.where(tril, gcol[p] - gr, 0.0)), 0.0)
        amat[p] = jnp.where(strict, bcol[p] * qkk[a, c][c_len:] * dec[p], 0.0)
        dmat[p] = eye - jnp.where(lvl_masks[0], amat[p], 0.0)

    for mask in lvl_masks[1:]:
        db = {p: dmat[p].astype(BF16) for p in pairs}
        x = {p: _dot(db[p], jnp.where(mask, amat[p], 0.0).astype(BF16)) for p in pairs}
        for p in pairs:
            dmat[p] = dmat[p] - _dot(x[p].astype(BF16), db[p])

    eg, rhs = {}, {}
    for p in pairs:
        a, i, c = p
        eg[p] = jnp.exp(gcol[p])
        vc = y[rows[c], 2 * qw + (a * rep + i) * hd:2 * qw + (a * rep + i + 1) * hd]
        rhs[p] = jnp.concatenate([bcol[p] * vc, (bcol[p] * eg[p]) * kn[a][rows[c]]], axis=1)
    sol = {p: (rhs[p] + _dot((dmat[p] - eye).astype(BF16), rhs[p].astype(BF16))).astype(BF16) for p in pairs}
    att = {p: _dot((qkk[p[0], p[2]][:c_len] * dec[p]).astype(BF16), sol[p]) for p in pairs}
    glast, ks = {}, {}
    for p in pairs:
        a, i, c = p
        glast[p] = gcol[p][c_len - 1:c_len, :]
        kd = (kn[a][rows[c]] * jnp.exp(glast[p] - gcol[p])).astype(BF16)
        ks[p] = pl.dot(kd, sol[p], trans_a=True)
    lhs2 = {}
    for p in pairs:
        a, i, c = p
        qeff = qn[a][rows[c]] * eg[p] - att[p][:, hd:]
        lhs2[p] = jnp.concatenate([qeff.astype(BF16), ks[p][:, hd:].astype(BF16)], axis=0)

    states = [s_ref[hl] for hl in range(hs)]
    for c in range(nc):
        r = {}
        for a in range(kh):
            for i in range(rep):
                hl = a * rep + i
                r[hl] = _dot(lhs2[a, i, c], states[hl].astype(BF16))
        for a in range(kh):
            for i in range(rep):
                hl = a * rep + i
                p = (a, i, c)
                o = att[p][:, :hd] + r[hl][:c_len]
                states[hl] = states[hl] * jnp.exp(glast[p]) - r[hl][c_len:] + ks[p][:, :hd]
                o_ref[rows[c], hl * hd:(hl + 1) * hd] = o.astype(BF16)
    for hl in range(hs):
        s_ref[hl] = states[hl]


def _delta(proj, gcol, grow, conv_w, hd, batch, seq, k_heads, v_heads):
    m = proj.shape[0]
    rep = v_heads // k_heads
    hs = SUBLANES
    kh = hs // rep
    assert v_heads == rep * k_heads and hs == kh * rep and k_heads % kh == 0 and 2 * v_heads == LANES
    tb = _tile(seq, 256, DN_CHUNK)
    nt = seq // tb
    qblk, vblk = kh * hd, hs * hd
    k0 = k_heads // kh
    v0 = 2 * k_heads // hs
    tok = lambda b, j, n: b * nt + n
    return pl.pallas_call(
        functools.partial(_delta_kernel, heads=v_heads, rep=rep, kh=kh, hd=hd),
        grid=(batch, k_heads // kh, nt),
        in_specs=[pl.BlockSpec((tb, qblk), lambda b, j, n: (tok(b, j, n), j)),
                  pl.BlockSpec((tb, qblk), lambda b, j, n: (tok(b, j, n), k0 + j)),
                  pl.BlockSpec((tb, vblk), lambda b, j, n: (tok(b, j, n), v0 + j)),
                  pl.BlockSpec((tb, 2 * v_heads), lambda b, j, n: (tok(b, j, n), 0)),
                  pl.BlockSpec((hs, tb), lambda b, j, n: (j, tok(b, j, n))),
                  pl.BlockSpec((DN_CONV, qblk), lambda b, j, n: (0, j)),
                  pl.BlockSpec((DN_CONV, qblk), lambda b, j, n: (0, k0 + j)),
                  pl.BlockSpec((DN_CONV, vblk), lambda b, j, n: (0, v0 + j))],
        out_specs=pl.BlockSpec((tb, vblk), lambda b, j, n: (tok(b, j, n), j)),
        out_shape=jax.ShapeDtypeStruct((m, v_heads * hd), BF16),
        scratch_shapes=[pltpu.VMEM((tb + CONV_HALO, 2 * qblk + vblk), F32),
                        pltpu.VMEM((hs, hd, hd), F32)],
        compiler_params=_params(("parallel", "parallel", "arbitrary")),
        name="delta_rule",
    )(proj, proj, proj, gcol, grow, conv_w, conv_w, conv_w)


def kernel(x, norm_w, sg_w_in, sg_ln_w, sg_ln_b, sg_w_s, sg_b_s, sg_w_out, dn_w_in, dn_conv_w, dn_a_log,
           dn_dt_bias, dn_norm_w, dn_w_out, mlp_w_up, mlp_w_down):
    batch, seq, d = x.shape
    m = batch * seq
    depth = norm_w.shape[0]
    h = x.reshape(m, d)
    sg_w_in, sg_w_out, dn_w_in, dn_w_out, mlp_w_up, mlp_w_down = (
        w.astype(BF16) for w in (sg_w_in, sg_w_out, dn_w_in, dn_w_out, mlp_w_up, mlp_w_down))
    t = _rms_cast(h, norm_w[0, 0])
    for i in range(depth):
        j = i // 2
        if i % 2 == 0:
            z = _mm_act(t, sg_w_in, j, sg_w_in.shape[2], "gelu", "sg_in_gelu")
            mix = _sgu(z, sg_ln_w[j], sg_ln_b[j], sg_w_s[j], sg_b_s[j])
            w_out, gate = sg_w_out, None
        else:
            v_heads = dn_a_log.shape[1]
            hd = dn_norm_w.shape[1]
            v_dim = dn_w_out.shape[1]
            conv_ch = dn_conv_w.shape[2]
            k_heads = (conv_ch - v_dim) // (2 * hd)
            n_main = conv_ch + v_dim
            proj = _mm_act(t, dn_w_in, j, n_main, None, "dn_in")
            gcol, grow = _gates(t, dn_w_in, j, n_main, dn_a_log[j], dn_dt_bias[j])
            mix = _delta(proj, gcol, grow, dn_conv_w[j], hd, batch, seq, k_heads, v_heads)
            w_out, gate = dn_w_out, (proj, conv_ch, dn_norm_w[j])
        h, t = _mm_resnorm(mix, w_out, j, h, norm_w[i, 1], norm_w[i, 2], "mix_out", gate)
        hid = _mm_act(t, mlp_w_up, i, mlp_w_up.shape[2], "relu2", "mlp_up")
        w_next = norm_w[i + 1, 0] if i + 1 < depth else None
        h, t = _mm_resnorm(hid, mlp_w_down, i, h, norm_w[i, 3], w_next, "mlp_down")
    return h.reshape(batch, seq, d)
```

```python
import functools
import math

import jax
import jax.numpy as jnp
from jax import lax
from jax.experimental import pallas as pl
from jax.experimental.pallas import tpu as pltpu

NORM_EPS = 1e-6
LN_EPS = 1e-5
L2_EPS = 1e-6
SG_CHUNK = 128
DN_CHUNK = 64
DN_CONV = 4
CONV_HALO = 8
SUBLANES = 8
LANES = 128
V7X_VMEM_LIMIT_BYTES = 60000 * 1024

F32 = jnp.float32
BF16 = jnp.bfloat16


def _tile(dim, pref, align=LANES):
    if dim <= pref:
        return dim
    t = (pref // align) * align
    while t > align and dim % t:
        t -= align
    assert dim % t == 0, (dim, pref)
    return t


def _params(sem):
    return pltpu.CompilerParams(dimension_semantics=sem, vmem_limit_bytes=V7X_VMEM_LIMIT_BYTES)


def _sigmoid(x):
    return 1.0 / (1.0 + jnp.exp(-x))


def _softplus(x):
    return jnp.maximum(x, 0.0) + jnp.log1p(jnp.exp(-jnp.abs(x)))


def _dot(a, b):
    return jnp.dot(a, b, preferred_element_type=F32)


def _rms_cast_kernel(x_ref, w_ref, o_ref):
    x = x_ref[...]
    ms = jnp.mean(x * x, axis=-1, keepdims=True)
    o_ref[...] = (x * lax.rsqrt(ms + NORM_EPS) * w_ref[...]).astype(o_ref.dtype)


def _rms_cast(x, w):
    m, d = x.shape
    tm = _tile(m, 256, 8)
    return pl.pallas_call(
        _rms_cast_kernel,
        grid=(m // tm,),
        in_specs=[pl.BlockSpec((tm, d), lambda i: (i, 0)),
                  pl.BlockSpec((1, d), lambda i: (0, 0))],
        out_specs=pl.BlockSpec((tm, d), lambda i: (i, 0)),
        out_shape=jax.ShapeDtypeStruct((m, d), BF16),
        compiler_params=_params(("parallel",)),
        name="rms_cast",
    )(x, w.reshape(1, d))


def _mm_act_kernel(x_ref, w_ref, o_ref, *, act):
    acc = _dot(x_ref[...], w_ref[...])
    if act == "gelu":
        acc = 0.5 * acc * (1.0 + lax.erf(acc * math.sqrt(0.5)))
    elif act == "relu2":
        r = jnp.maximum(acc, 0.0)
        acc = r * r
    o_ref[...] = acc.astype(o_ref.dtype)


def _mm_act(x, w, layer, n, act, name):
    m, k = x.shape
    tm = _tile(m, 1024)
    tn = _tile(n, 1024)
    return pl.pallas_call(
        functools.partial(_mm_act_kernel, act=act),
        grid=(m // tm, n // tn),
        in_specs=[pl.BlockSpec((tm, k), lambda i, j: (i, 0)),
                  pl.BlockSpec((None, k, tn), lambda i, j: (layer, 0, j))],
        out_specs=pl.BlockSpec((tm, tn), lambda i, j: (i, j)),
        out_shape=jax.ShapeDtypeStruct((m, n), BF16),
        compiler_params=_params(("parallel", "arbitrary")),
        name=name,
    )(x, w)


def _mm_resnorm_kernel(x_ref, *refs, nk, col_chunk, row_chunk, gate_hd):
    if gate_hd:
        zg_ref, nw_ref, w_ref, res_ref, wpost_ref, wnext_ref, h_ref, *t_ref = refs
    else:
        w_ref, res_ref, wpost_ref, wnext_ref, h_ref, *t_ref = refs
    kk = pl.program_id(1)
    tm, n = h_ref.shape

    @pl.when(kk == 0)
    def _():
        h_ref[...] = jnp.zeros_like(h_ref)

    x = x_ref[...]
    if gate_hd:
        parts = []
        for g in range(x.shape[1] // gate_hd):
            cols = slice(g * gate_hd, (g + 1) * gate_hd)
            o = x[:, cols].astype(F32)
            zc = zg_ref[:, cols].astype(F32)
            on = o * lax.rsqrt(jnp.mean(o * o, axis=-1, keepdims=True) + NORM_EPS) * nw_ref[...] * (zc * _sigmoid(zc))
            parts.append(on.astype(BF16))
        x = jnp.concatenate(parts, axis=1)
    for c in range(n // col_chunk):
        sl = slice(c * col_chunk, (c + 1) * col_chunk)
        h_ref[:, sl] += _dot(x, w_ref[:, sl])

    @pl.when(kk == nk - 1)
    def _():
        def body(r, carry):
            rows = pl.ds(pl.multiple_of(r * row_chunk, row_chunk), row_chunk)
            mval = h_ref[rows, :]
            ms = jnp.mean(mval * mval, axis=-1, keepdims=True)
            h = res_ref[rows, :] + mval * lax.rsqrt(ms + NORM_EPS) * wpost_ref[...]
            h_ref[rows, :] = h
            if t_ref:
                hs = jnp.mean(h * h, axis=-1, keepdims=True)
                t_ref[0][rows, :] = (h * lax.rsqrt(hs + NORM_EPS) * wnext_ref[...]).astype(BF16)
            return carry

        lax.fori_loop(0, tm // row_chunk, body, 0, unroll=4)


def _mm_resnorm(x, w, layer, res, w_post, w_next, name, gate=None):
    m, k = x.shape
    n = w.shape[2]
    tm = _tile(m, 512)
    deep = k >= 4 * n
    tk = _tile(k, 1024 if deep else 512)
    nk = k // tk
    emit_t = w_next is not None
    wn = (w_next if emit_t else w_post).reshape(1, n)
    out_shape = [jax.ShapeDtypeStruct((m, n), F32)]
    out_specs = [pl.BlockSpec((tm, n), lambda i, j: (i, 0))]
    if emit_t:
        out_shape.append(jax.ShapeDtypeStruct((m, n), BF16))
        out_specs.append(pl.BlockSpec((tm, n), lambda i, j: (i, 0)))
    gate_specs, gate_args, gate_hd = [], [], 0
    if gate is not None:
        z_src, z_col0, head_w = gate
        gate_hd = head_w.shape[0]
        assert z_col0 % tk == 0 and tk % gate_hd == 0
        gate_specs = [pl.BlockSpec((tm, tk), lambda i, j: (i, z_col0 // tk + j)),
                      pl.BlockSpec((1, gate_hd), lambda i, j: (0, 0))]
        gate_args = [z_src, head_w.reshape(1, gate_hd)]
    outs = pl.pallas_call(
        functools.partial(_mm_resnorm_kernel, nk=nk, col_chunk=_tile(n, 1024), row_chunk=_tile(tm, 32, 8),
                          gate_hd=gate_hd),
        grid=(m // tm, nk),
        in_specs=[pl.BlockSpec((tm, tk), lambda i, j: (i, j))] + gate_specs + [
                  pl.BlockSpec((None, tk, n), lambda i, j: (layer, j, 0)),
                  pl.BlockSpec((tm, n), lambda i, j: (i, 0), pipeline_mode=pl.Buffered(1 if deep else 2)),
                  pl.BlockSpec((1, n), lambda i, j: (0, 0)),
                  pl.BlockSpec((1, n), lambda i, j: (0, 0))],
        out_specs=out_specs,
        out_shape=out_shape,
        compiler_params=_params(("parallel", "arbitrary")),
        name=name,
    )(x, *gate_args, w, res, w_post.reshape(1, n), wn)
    return (outs[0], outs[1]) if emit_t else (outs[0], None)


def _sgu_kernel(u_ref, v_ref, lnw_ref, lnb_ref, ws_ref, bs_ref, o_ref, vn_ref, *, groups, row_chunk):
    tm, e = v_ref.shape
    gd = e // groups

    def ln_body(r, carry):
        rows = pl.ds(pl.multiple_of(r * row_chunk, row_chunk), row_chunk)
        v = v_ref[rows, :].astype(F32)
        mu = jnp.mean(v, axis=-1, keepdims=True)
        xc = v - mu
        var = jnp.mean(xc * xc, axis=-1, keepdims=True)
        vn_ref[rows, :] = (xc * lax.rsqrt(var + LN_EPS) * lnw_ref[...] + lnb_ref[...]).astype(BF16)
        return carry

    lax.fori_loop(0, tm // row_chunk, ln_body, 0, unroll=2)

    ri = lax.broadcasted_iota(jnp.int32, (SG_CHUNK, SG_CHUNK), 0)
    ci = lax.broadcasted_iota(jnp.int32, (SG_CHUNK, SG_CHUNK), 1)
    causal = ri >= ci

    for g in range(groups):
        w = jnp.where(causal, ws_ref[g], jnp.zeros((), BF16))
        bias = bs_ref[g]
        cols = slice(g * gd, (g + 1) * gd)
        for c in range(tm // SG_CHUNK):
            rows = slice(c * SG_CHUNK, (c + 1) * SG_CHUNK)
            s = _dot(w, vn_ref[rows, cols]) + bias
            o_ref[rows, cols] = (u_ref[rows, cols].astype(F32) * s).astype(BF16)


def _sgu(z, ln_w, ln_b, w_s, b_s):
    m, e2 = z.shape
    e = e2 // 2
    groups, c, _ = w_s.shape
    assert c == SG_CHUNK
    tm = _tile(m, 256, SG_CHUNK)
    return pl.pallas_call(
        functools.partial(_sgu_kernel, groups=groups, row_chunk=_tile(tm, 32, 8)),
        grid=(m // tm,),
        in_specs=[pl.BlockSpec((tm, e), lambda i: (i, 0)),
                  pl.BlockSpec((tm, e), lambda i: (i, 1)),
                  pl.BlockSpec((1, e), lambda i: (0, 0)),
                  pl.BlockSpec((1, e), lambda i: (0, 0)),
                  pl.BlockSpec((groups, c, c), lambda i: (0, 0, 0)),
                  pl.BlockSpec((groups, c, 1), lambda i: (0, 0, 0))],
        out_specs=pl.BlockSpec((tm, e), lambda i: (i, 0)),
        out_shape=jax.ShapeDtypeStruct((m, e), BF16),
        scratch_shapes=[pltpu.VMEM((tm, e), BF16)],
        compiler_params=_params(("parallel",)),
        name="sgu",
    )(z, z, ln_w.reshape(1, e), ln_b.reshape(1, e), w_s.astype(BF16), b_s.reshape(groups, c, 1))


def _chunk_cumsum(x, axis):
    pos = lax.broadcasted_iota(jnp.int32, x.shape, axis) % DN_CHUNK
    sh = 1
    while sh < DN_CHUNK:
        x = x + jnp.where(pos >= sh, pltpu.roll(x, sh, axis), 0.0)
        sh *= 2
    return x


def _gates_kernel(t_ref, w_ref, ac_ref, dc_ref, gcol_ref, grow_ref, *, heads):
    ba = _dot(t_ref[...], w_ref[...])
    beta = _sigmoid(ba)
    g = -jnp.exp(ac_ref[...]) * _softplus(ba + dc_ref[...])
    gc = _chunk_cumsum(g, 0)
    lane = lax.broadcasted_iota(jnp.int32, ba.shape, 1)
    gcol_ref[...] = jnp.where(lane < heads, beta, gc)
    grow_ref[...] = gc.T[heads:, :]


def _gates(t, w, layer, col0, a_log, dt_bias):
    m, d = t.shape
    heads = a_log.shape[0]
    assert 2 * heads == LANES and col0 % LANES == 0
    tm = _tile(m, 512)
    zeros = jnp.zeros((heads,), F32)
    ac = jnp.concatenate([zeros, a_log]).reshape(1, 2 * heads)
    dc = jnp.concatenate([zeros, dt_bias]).reshape(1, 2 * heads)
    return pl.pallas_call(
        functools.partial(_gates_kernel, heads=heads),
        grid=(m // tm,),
        in_specs=[pl.BlockSpec((tm, d), lambda i: (i, 0)),
                  pl.BlockSpec((None, d, 2 * heads), lambda i: (layer, 0, col0 // LANES)),
                  pl.BlockSpec((1, 2 * heads), lambda i: (0, 0)),
                  pl.BlockSpec((1, 2 * heads), lambda i: (0, 0))],
        out_specs=[pl.BlockSpec((tm, 2 * heads), lambda i: (i, 0)),
                   pl.BlockSpec((heads, tm), lambda i: (0, i))],
        out_shape=[jax.ShapeDtypeStruct((m, 2 * heads), F32),
                   jax.ShapeDtypeStruct((heads, m), F32)],
        compiler_params=_params(("parallel",)),
        name="dn_gates",
    )(t, w, ac, dc)


def _delta_kernel(q_ref, k_ref, v_ref, gcol_ref, grow_ref, cwq_ref, cwk_ref, cwv_ref,
                  o_ref, xpad_ref, s_ref, *, heads, rep, kh, hd):
    j = pl.program_id(1)
    n = pl.program_id(2)
    tb = q_ref.shape[0]
    hs = kh * rep
    qw = kh * hd
    width = 2 * qw + hs * hd
    c_len = DN_CHUNK
    nc = tb // c_len

    @pl.when(n == 0)
    def _():
        xpad_ref[...] = jnp.zeros_like(xpad_ref)
        s_ref[...] = jnp.zeros_like(s_ref)

    cw = jnp.concatenate([cwq_ref[...], cwk_ref[...], cwv_ref[...]], axis=1)
    xb = jnp.concatenate([q_ref[...], k_ref[...], v_ref[...]], axis=1)
    prev = xpad_ref[...]
    sr = lax.broadcasted_iota(jnp.int32, (tb, tb), 0)
    sc = lax.broadcasted_iota(jnp.int32, (tb, tb), 1)
    hrow = lax.broadcasted_iota(jnp.int32, (CONV_HALO, width), 0)
    xf = xb.astype(F32)
    y = cw[DN_CONV - 1:DN_CONV, :] * xf
    for sh in range(1, DN_CONV):
        shifted = _dot((sr - sc == sh).astype(BF16), xb)
        head = jnp.where(hrow < sh, pltpu.roll(prev, sh, 0), shifted[0:CONV_HALO])
        shifted = jnp.concatenate([head, shifted[CONV_HALO:]], axis=0)
        y = y + cw[DN_CONV - 1 - sh:DN_CONV - sh, :] * shifted
    xpad_ref[...] = xf[tb - CONV_HALO:, :]
    y = y * _sigmoid(y)

    qn, kn = [], []
    for a in range(kh):
        qa = y[:, a * hd:(a + 1) * hd]
        ka = y[:, qw + a * hd:qw + (a + 1) * hd]
        qn.append(qa * lax.rsqrt(jnp.sum(qa * qa, axis=-1, keepdims=True) + L2_EPS) * (hd ** -0.5))
        kn.append(ka * lax.rsqrt(jnp.sum(ka * ka, axis=-1, keepdims=True) + L2_EPS))

    gsel = pltpu.roll(gcol_ref[...], (2 * heads - hs * j) % (2 * heads), 1)
    grow = grow_ref[...]

    ri = lax.broadcasted_iota(jnp.int32, (c_len, c_len), 0)
    ci = lax.broadcasted_iota(jnp.int32, (c_len, c_len), 1)
    tril = ri >= ci
    strict = ri > ci
    eye = (ri == ci).astype(F32)
    lvl_masks = []
    s = 1
    while s < c_len:
        lvl_masks.append(((ri // (2 * s)) == (ci // (2 * s))) & ((ri // s) != (ci // s)) & strict)
        s *= 2

    pairs =[(a, i, c) for c in range(nc) for a in range(kh) for i in range(rep)]
    rows = [slice(c * c_len, (c + 1) * c_len) for c in range(nc)]

    qkk = {}
    for c in range(nc):
        for a in range(kh):
            kb = kn[a][rows[c]].astype(BF16)
            lhs = jnp.concatenate([qn[a][rows[c]].astype(BF16), kb], axis=0)
            qkk[a, c] = pl.dot(lhs, kb, trans_b=True)

    bcol, gcol, dec, amat, dmat = {}, {}, {}, {}, {}
    for p in pairs:
        a, i, c = p
        hl = a * rep + i
        bcol[p] = gsel[rows[c], hl:hl + 1]
        gcol[p] = gsel[rows[c], heads + hl:heads + hl + 1]
        gr = grow[hl:hl + 1, rows[c]]
        dec[p] = jnp.where(tril, jnp.exp(jnp.where(tril, gcol[p] - gr, 0.0)), 0.0)
        amat[p] = jnp.where(strict, bcol[p] * qkk[a, c][c_len:] * dec[p], 0.0)
        dmat[p] = eye - jnp.where(lvl_masks[0], amat[p], 0.0)

    for mask in lvl_masks[1:]:
        db = {p: dmat[p].astype(BF16) for p in pairs}
        x = {p: _dot(db[p], jnp.where(mask, amat[p], 0.0).astype(BF16)) for p in pairs}
        for p in pairs:
            dmat[p] = dmat[p] - _dot(x[p].astype(BF16), db[p])

    eg, rhs = {}, {}
    for p in pairs:
        a, i, c = p
        eg[p] = jnp.exp(gcol[p])
        vc = y[rows[c], 2 * qw + (a * rep + i) * hd:2 * qw + (a * rep + i + 1) * hd]
        rhs[p] = jnp.concatenate([bcol[p] * vc, (bcol[p] * eg[p]) * kn[a][rows[c]]], axis=1)
    sol = {p: (rhs[p] + _dot((dmat[p] - eye).astype(BF16), rhs[p].astype(BF16))).astype(BF16) for p in pairs}
    att = {p: _dot((qkk[p[0], p[2]][:c_len] * dec[p]).astype(BF16), sol[p]) for p in pairs}
    glast, ks = {}, {}
    for p in pairs:
        a, i, c = p
        glast[p] = gcol[p][c_len - 1:c_len, :]
        kd = (kn[a][rows[c]] * jnp.exp(glast[p] - gcol[p])).astype(BF16)
        ks[p] = pl.dot(kd, sol[p], trans_a=True)
    lhs2 = {}
    for p in pairs:
        a, i, c = p
        qeff = qn[a][rows[c]] * eg[p] - att[p][:, hd:]
        lhs2[p] = jnp.concatenate([qeff.astype(BF16), ks[p][:, hd:].astype(BF16)], axis=0)

    states = [s_ref[hl] for hl in range(hs)]
    for c in range(nc):
        r = {}
        for a in range(kh):
            for i in range(rep):
                hl = a * rep + i
                r[hl] = _dot(lhs2[a, i, c], states[hl].astype(BF16))
        for a in range(kh):
            for i in range(rep):
                hl = a * rep + i
                p = (a, i, c)
                o = att[p][:, :hd] + r[hl][:c_len]
                states[hl] = states[hl] * jnp.exp(glast[p]) - r[hl][c_len:] + ks[p][:, :hd]
                o_ref[rows[c], hl * hd:(hl + 1) * hd] = o.astype(BF16)
    for hl in range(hs):
        s_ref[hl] = states[hl]


def _delta(proj, gcol, grow, conv_w, hd, batch, seq, k_heads, v_heads):
    m = proj.shape[0]
    rep = v_heads // k_heads
    hs = SUBLANES
    kh = hs // rep
    assert v_heads == rep * k_heads and hs == kh * rep and k_heads % kh == 0 and 2 * v_heads == LANES
    tb = _tile(seq, 256, DN_CHUNK)
    nt = seq // tb
    qblk, vblk = kh * hd, hs * hd
    k0 = k_heads // kh
    v0 = 2 * k_heads // hs
    tok = lambda b, j, n: b * nt + n
    return pl.pallas_call(
        functools.partial(_delta_kernel, heads=v_heads, rep=rep, kh=kh, hd=hd),
        grid=(batch, k_heads // kh, nt),
        in_specs=[pl.BlockSpec((tb, qblk), lambda b, j, n: (tok(b, j, n), j)),
                  pl.BlockSpec((tb, qblk), lambda b, j, n: (tok(b, j, n), k0 + j)),
                  pl.BlockSpec((tb, vblk), lambda b, j, n: (tok(b, j, n), v0 + j)),
                  pl.BlockSpec((tb, 2 * v_heads), lambda b, j, n: (tok(b, j, n), 0)),
                  pl.BlockSpec((hs, tb), lambda b, j, n: (j, tok(b, j, n))),
                  pl.BlockSpec((DN_CONV, qblk), lambda b, j, n: (0, j)),
                  pl.BlockSpec((DN_CONV, qblk), lambda b, j, n: (0, k0 + j)),
                  pl.BlockSpec((DN_CONV, vblk), lambda b, j, n: (0, v0 + j))],
        out_specs=pl.BlockSpec((tb, vblk), lambda b, j, n: (tok(b, j, n), j)),
        out_shape=jax.ShapeDtypeStruct((m, v_heads * hd), BF16),
        scratch_shapes=[pltpu.VMEM((CONV_HALO, 2 * qblk + vblk), F32),
                        pltpu.VMEM((hs, hd, hd), F32)],
        compiler_params=_params(("parallel", "parallel", "arbitrary")),
        name="delta_rule",
    )(proj, proj, proj, gcol, grow, conv_w, conv_w, conv_w)


def kernel(x, norm_w, sg_w_in, sg_ln_w, sg_ln_b, sg_w_s, sg_b_s, sg_w_out, dn_w_in, dn_conv_w, dn_a_log,
           dn_dt_bias, dn_norm_w, dn_w_out, mlp_w_up, mlp_w_down):
    batch, seq, d = x.shape
    m = batch * seq
    depth = norm_w.shape[0]
    h = x.reshape(m, d)
    sg_w_in, sg_w_out, dn_w_in, dn_w_out, mlp_w_up, mlp_w_down = (
        w.astype(BF16) for w in (sg_w_in, sg_w_out, dn_w_in, dn_w_out, mlp_w_up, mlp_w_down))
    t = _rms_cast(h, norm_w[0, 0])
    for i in range(depth):
        j = i // 2
        if i % 2 == 0:
            z = _mm_act(t, sg_w_in, j, sg_w_in.shape[2], "gelu", "sg_in_gelu")
            mix = _sgu(z, sg_ln_w[j], sg_ln_b[j], sg_w_s[j], sg_b_s[j])
            w_out, gate = sg_w_out, None
        else:
            v_heads = dn_a_log.shape[1]
            hd = dn_norm_w.shape[1]
            v_dim = dn_w_out.shape[1]
            conv_ch = dn_conv_w.shape[2]
            k_heads = (conv_ch - v_dim) // (2 * hd)
            n_main = conv_ch + v_dim
            proj = _mm_act(t, dn_w_in, j, n_main, None, "dn_in")
            gcol, grow = _gates(t, dn_w_in, j, n_main, dn_a_log[j], dn_dt_bias[j])
            mix = _delta(proj, gcol, grow, dn_conv_w[j], hd, batch, seq, k_heads, v_heads)
            w_out, gate = dn_w_out, (proj, conv_ch, dn_norm_w[j])
        h, t = _mm_resnorm(mix, w_out, j, h, norm_w[i, 1], norm_w[i, 2], "mix_out", gate)
        hid = _mm_act(t, mlp_w_up, i, mlp_w_up.shape[2], "relu2", "mlp_up")
        w_next = norm_w[i + 1, 0] if i + 1 < depth else None
        h, t = _mm_resnorm(hid, mlp_w_down, i, h, norm_w[i, 3], w_next, "mlp_down")
    return h.reshape(batch, seq, d)
```
